```python
import functools
import jax, jax.numpy as jnp
from jax import lax
import numpy as np

D_MODEL = 2048
BATCH = 2
SEQ = 4096
DEPTH = 1
DEC_BATCH = 128
DEC_SEQ = 1
PAST_LEN = 16384
PAGE_SIZE = 128

MLA_HEADS = 8
QK_NOPE = 128
QK_ROPE = 64
V_HEAD = 128
KV_RANK = 512
ROPE_THETA = 10000.0
Q_BLOCK = 128
RW_HEADS = 16
RW_HEAD = 64
RW_WIDTH = RW_HEADS * RW_HEAD
DECAY_LORA = 64
AAA_LORA = 64
GATE_LORA = 160
RW_COLS = 3 * RW_WIDTH + DECAY_LORA + AAA_LORA + GATE_LORA
Q_COLS = MLA_HEADS * (QK_NOPE + QK_ROPE)
MLA_COLS = Q_COLS + KV_RANK + QK_ROPE
IN_COLS = MLA_COLS + RW_COLS
MIX_WIDTH = MLA_HEADS * V_HEAD + RW_WIDTH
PEER_HEADS = 8
N_KEYS = 128
N_EXPERTS = N_KEYS * N_KEYS
PK_DIM = 256
PK_HALF = PK_DIM // 2
PK_TOPK = 16
PEER_BLOCK = 128
PLE_DIM = 256
ALPHA = (2 * DEPTH) ** 0.25
BETA = (8 * DEPTH) ** -0.25
LN_EPS = 1e-5
RMS_EPS = 1e-6
GN_EPS = 64e-5

kernel_name = "hymba_mla_rwkv7_peer_deepnorm_step"


def layer_norm(x, g, b):
    xf = x.astype(jnp.float32)
    mu = xf.mean(-1, keepdims=True)
    var = jnp.square(xf - mu).mean(-1, keepdims=True)
    return ((xf - mu) * lax.rsqrt(var + LN_EPS) * g + b).astype(x.dtype)


def rms_norm(x, g):
    xf = x.astype(jnp.float32)
    return (xf * lax.rsqrt(jnp.mean(xf * xf, -1, keepdims=True) + RMS_EPS) * g).astype(x.dtype)


def rope_tables(pos):
    inv = ROPE_THETA ** (-jnp.arange(0, QK_ROPE, 2, dtype=jnp.float32) / QK_ROPE)
    ang = pos.astype(jnp.float32)[:, None] * inv[None, :]
    return jnp.cos(ang), jnp.sin(ang)


def apply_rope(x, cos, sin):
    half = QK_ROPE // 2
    x1 = x[..., :half].astype(jnp.float32)
    x2 = x[..., half:].astype(jnp.float32)
    return jnp.concatenate([x1 * cos - x2 * sin, x2 * cos + x1 * sin], -1).astype(x.dtype)


def mla_project(p_mla, pos, kv_norm_g):
    B, T, _ = p_mla.shape
    q = p_mla[..., :Q_COLS].reshape(B, T, MLA_HEADS, QK_NOPE + QK_ROPE)
    q_nope, q_rope = q[..., :QK_NOPE], q[..., QK_NOPE:]
    c_kv = rms_norm(p_mla[..., Q_COLS:Q_COLS + KV_RANK], kv_norm_g)
    k_rope = p_mla[..., Q_COLS + KV_RANK:]
    cos, sin = rope_tables(pos)
    q_rope = apply_rope(q_rope, cos[:, None, :], sin[:, None, :])
    k_rope = apply_rope(k_rope, cos, sin)
    return q_nope, q_rope, c_kv, k_rope


def mla_prompt(q_nope, q_rope, c_kv, k_rope, w_uk, w_uv):
    B, S, _ = c_kv.shape
    scale = (QK_NOPE + QK_ROPE) ** -0.5
    k_nope = jnp.einsum('bsc,chd->bshd', c_kv, w_uk)
    v = jnp.einsum('bsc,chd->bshd', c_kv, w_uv)
    nqb = S // Q_BLOCK
    qn_b = q_nope.reshape(B, nqb, Q_BLOCK, MLA_HEADS, QK_NOPE).transpose(1, 0, 2, 3, 4)
    qr_b = q_rope.reshape(B, nqb, Q_BLOCK, MLA_HEADS, QK_ROPE).transpose(1, 0, 2, 3, 4)
    kpos = jnp.arange(S)

    def block(args):
        qn, qr, i = args
        s = (jnp.einsum('bqhd,bkhd->bhqk', qn, k_nope, preferred_element_type=jnp.float32)
             + jnp.einsum('bqhr,bkr->bhqk', qr, k_rope, preferred_element_type=jnp.float32)) * scale
        qpos = i * Q_BLOCK + jnp.arange(Q_BLOCK)
        s = jnp.where(kpos[None, :] <= qpos[:, None], s, -jnp.inf)
        p = jax.nn.softmax(s, axis=-1).astype(v.dtype)
        return jnp.einsum('bhqk,bkhd->bqhd', p, v)

    o = lax.map(block, (qn_b, qr_b, jnp.arange(nqb)))
    return o.transpose(1, 0, 2, 3, 4).reshape(B, S, MLA_HEADS * V_HEAD)


def mla_sample(q_nope, q_rope, c_new, kr_new, w_uk, w_uv, cache_ckv, cache_krope, page_table):
    DB, T, _ = c_new.shape
    scale = (QK_NOPE + QK_ROPE) ** -0.5
    c_past = cache_ckv[page_table].reshape(DB, -1, KV_RANK)
    kr_past = cache_krope[page_table].reshape(DB, -1, QK_ROPE)
    n_past = c_past.shape[1]
    q_lat = jnp.einsum('bthd,chd->bthc', q_nope, w_uk)
    s_past = (jnp.einsum('bthc,bpc->bhtp', q_lat, c_past, preferred_element_type=jnp.float32)
              + jnp.einsum('bthr,bpr->bhtp', q_rope, kr_past, preferred_element_type=jnp.float32)) * scale
    s_new = (jnp.einsum('bthc,bpc->bhtp', q_lat, c_new, preferred_element_type=jnp.float32)
             + jnp.einsum('bthr,bpr->bhtp', q_rope, kr_new, preferred_element_type=jnp.float32)) * scale
    s_new = jnp.where(jnp.tril(jnp.ones((T, T), dtype=bool)), s_new, -jnp.inf)
    p = jax.nn.softmax(jnp.concatenate([s_past, s_new], -1), axis=-1).astype(c_new.dtype)
    o_lat = (jnp.einsum('bhtp,bpc->bthc', p[..., :n_past], c_past)
             + jnp.einsum('bhtp,bpc->bthc', p[..., n_past:], c_new))
    return jnp.einsum('bthc,chd->bthd', o_lat, w_uv).reshape(DB, T, MLA_HEADS * V_HEAD)


def rwkv_time_mix(p_rw, shift0, wkv0, rw_mu, rw_w0, rw_w_up, rw_a0, rw_a_up, rw_g_up,
                  rw_k_k, rw_k_a, rw_r_k, rw_gn_g, rw_gn_b):
    B, T, _ = p_rw.shape
    f32 = jnp.float32
    prev = jnp.concatenate([shift0[:, None, :].astype(p_rw.dtype), p_rw[:, :-1]], axis=1)
    xs = p_rw + rw_mu * (prev - p_rw)
    r, k, v, xw, xa, xg = jnp.split(
        xs, [RW_WIDTH, 2 * RW_WIDTH, 3 * RW_WIDTH, 3 * RW_WIDTH + DECAY_LORA,
             3 * RW_WIDTH + DECAY_LORA + AAA_LORA], axis=-1)
    w = -jax.nn.softplus(-(rw_w0 + jnp.tanh(xw) @ rw_w_up)) - 0.5
    decay = jnp.exp(-jnp.exp(w.astype(f32)))
    a = jax.nn.sigmoid(rw_a0 + xa @ rw_a_up)
    g = jax.nn.sigmoid(xg) @ rw_g_up
    hd = lambda t: t.astype(f32).reshape(B, T, RW_HEADS, RW_HEAD)
    kk = hd(k * rw_k_k)
    kk = kk * lax.rsqrt(jnp.maximum(jnp.sum(kk * kk, -1, keepdims=True), 1e-24))
    k = k * (1 + (a - 1) * rw_k_a)
    rh, kh, vh, ah, wh = hd(r), hd(k), hd(v), hd(a), hd(decay)

    def step(S, inp):
        r_t, w_t, k_t, v_t, kk_t, a_t = inp
        sa = jnp.einsum('bhij,bhj->bhi', S, -kk_t)
        S = (S * w_t[:, :, None, :] + sa[..., None] * (kk_t * a_t)[:, :, None, :]
             + v_t[..., None] * k_t[:, :, None, :])
        return S, jnp.einsum('bhij,bhj->bhi', S, r_t)

    tmaj = lambda t: t.transpose(1, 0, 2, 3)
    wkv_T, out = lax.scan(step, wkv0.astype(f32),
                          (tmaj(rh), tmaj(wh), tmaj(kh), tmaj(vh), tmaj(kk), tmaj(ah)))
    out = tmaj(out)
    mu = out.mean(-1, keepdims=True)
    var = jnp.square(out - mu).mean(-1, keepdims=True)
    out = ((out - mu) * lax.rsqrt(var + GN_EPS)).reshape(B, T, RW_WIDTH) * rw_gn_g + rw_gn_b
    bonus = (jnp.sum(rh * kh * rw_r_k, -1, keepdims=True) * vh).reshape(B, T, RW_WIDTH)
    out = (out + bonus) * g
    return out.astype(p_rw.dtype), wkv_T, p_rw[:, -1]


def peer(x, peer_wq, peer_k1, peer_k2, peer_u, peer_v):
    shp = x.shape
    xf = x.reshape(-1, D_MODEL)
    n_tok = xf.shape[0]
    nb = -(-n_tok // PEER_BLOCK)
    xf = jnp.pad(xf, ((0, nb * PEER_BLOCK - n_tok), (0, 0))).reshape(nb, PEER_BLOCK, D_MODEL)

    def block(xb):
        q = (xb @ peer_wq).reshape(PEER_BLOCK, PEER_HEADS, PK_DIM)
        s1 = jnp.einsum('thd,hnd->thn', q[..., :PK_HALF], peer_k1, preferred_element_type=jnp.float32)
        s2 = jnp.einsum('thd,hnd->thn', q[..., PK_HALF:], peer_k2, preferred_element_type=jnp.float32)
        v1, i1 = lax.top_k(s1, PK_TOPK)
        v2, i2 = lax.top_k(s2, PK_TOPK)
        cand = (v1[..., :, None] + v2[..., None, :]).reshape(PEER_BLOCK, PEER_HEADS, PK_TOPK * PK_TOPK)
        sc, ci = lax.top_k(cand, PK_TOPK)
        e = (jnp.take_along_axis(i1, ci // PK_TOPK, -1) * N_KEYS
             + jnp.take_along_axis(i2, ci % PK_TOPK, -1))
        gate = jax.nn.softmax(sc, axis=-1)
        hid = jax.nn.gelu(jnp.einsum('td,thkd->thk', xb, peer_u[e], preferred_element_type=jnp.float32),
                          approximate=False)
        return jnp.einsum('thk,thkd->td', (gate * hid).astype(xb.dtype), peer_v[e])

    y = lax.map(block, xf).reshape(-1, D_MODEL)[:n_tok]
    return y.reshape(shp)


def decoder_layer(x, p_emb, pos, wkv0, shift0, attend,
                  w_in, kv_norm_g, w_uk, w_uv, rw_mu, rw_w0, rw_w_up, rw_a0, rw_a_up, rw_g_up,
                  rw_k_k, rw_k_a, rw_r_k, rw_gn_g, rw_gn_b, w_o, ln1_g, ln1_b,
                  peer_wq, peer_k1, peer_k2, peer_u, peer_v, ln2_g, ln2_b,
                  ple_w, ple_gate_w, ln3_g, ln3_b):
    proj = x @ w_in
    q_nope, q_rope, c_kv, k_rope = mla_project(proj[..., :MLA_COLS], pos, kv_norm_g)
    o_mla = attend(q_nope, q_rope, c_kv, k_rope, w_uk, w_uv)
    o_rw, wkv_new, shift_new = rwkv_time_mix(proj[..., MLA_COLS:], shift0, wkv0, rw_mu, rw_w0, rw_w_up,
                                             rw_a0, rw_a_up, rw_g_up, rw_k_k, rw_k_a, rw_r_k,
                                             rw_gn_g, rw_gn_b)
    mixed = jnp.concatenate([o_mla, o_rw], -1) @ w_o
    x = layer_norm(ALPHA * x + mixed, ln1_g, ln1_b)
    x = layer_norm(ALPHA * x + peer(x, peer_wq, peer_k1, peer_k2, peer_u, peer_v), ln2_g, ln2_b)
    ple = (p_emb @ ple_w) * jax.nn.sigmoid(x @ ple_gate_w)
    x = layer_norm(ALPHA * x + ple, ln3_g, ln3_b)
    return x, c_kv, k_rope, wkv_new, shift_new


def setup_inputs(seed: int = 0) -> dict:
    key = jax.random.key(seed)
    ks = iter(jax.random.split(key, 48))
    nrm = lambda shape, s: jax.random.normal(next(ks), shape, jnp.float32) * s
    n_pages = PAST_LEN // PAGE_SIZE
    n_phys = (DEC_BATCH * n_pages * 5) // 4
    L = DEPTH
    page_table = jax.random.permutation(next(ks), n_phys)[:DEC_BATCH * n_pages]
    page_table = page_table.reshape(DEC_BATCH, n_pages).astype(jnp.int32)
    w0 = jnp.linspace(-6.0, -1.0, RW_WIDTH, dtype=jnp.float32)[None, :] + nrm((L, RW_WIDTH), 0.1)
    return {
        "x_prompt": nrm((BATCH, SEQ, D_MODEL), 1.0),
        "x_sample": nrm((DEC_BATCH, DEC_SEQ, D_MODEL), 1.0),
        "p_prompt": nrm((DEPTH, BATCH, SEQ, PLE_DIM), 1.0),
        "p_sample": nrm((DEPTH, DEC_BATCH, DEC_SEQ, PLE_DIM), 1.0),
        "cache_ckv": nrm((L, n_phys, PAGE_SIZE, KV_RANK), 1.0),
        "cache_krope": nrm((L, n_phys, PAGE_SIZE, QK_ROPE), 1.0),
        "state_wkv": nrm((L, DEC_BATCH, RW_HEADS, RW_HEAD, RW_HEAD), 0.1),
        "state_shift": nrm((L, DEC_BATCH, RW_COLS), 1.0),
        "page_table": page_table,
        "w_in": nrm((L, D_MODEL, IN_COLS), D_MODEL ** -0.5),
        "kv_norm_g": 1.0 + nrm((L, KV_RANK), 0.02),
        "w_uk": nrm((L, KV_RANK, MLA_HEADS, QK_NOPE), KV_RANK ** -0.5),
        "w_uv": nrm((L, KV_RANK, MLA_HEADS, V_HEAD), BETA * KV_RANK ** -0.5),
        "rw_mu": jax.random.uniform(next(ks), (L, RW_COLS), jnp.float32),
        "rw_w0": w0,
        "rw_w_up": nrm((L, DECAY_LORA, RW_WIDTH), 0.1 * DECAY_LORA ** -0.5),
        "rw_a0": nrm((L, RW_WIDTH), 0.1),
        "rw_a_up": nrm((L, AAA_LORA, RW_WIDTH), 0.1 * AAA_LORA ** -0.5),
        "rw_g_up": nrm((L, GATE_LORA, RW_WIDTH), GATE_LORA ** -0.5),
        "rw_k_k": 0.85 + nrm((L, RW_WIDTH), 0.02),
        "rw_k_a": 1.0 + nrm((L, RW_WIDTH), 0.02),
        "rw_r_k": nrm((L, RW_HEADS, RW_HEAD), 0.1),
        "rw_gn_g": 1.0 + nrm((L, RW_WIDTH), 0.02),
        "rw_gn_b": nrm((L, RW_WIDTH), 0.02),
        "w_o": nrm((L, MIX_WIDTH, D_MODEL), BETA * MIX_WIDTH ** -0.5),
        "ln1_g": 1.0 + nrm((L, D_MODEL), 0.02),
        "ln1_b": nrm((L, D_MODEL), 0.02),
        "peer_wq": nrm((L, D_MODEL, PEER_HEADS * PK_DIM), D_MODEL ** -0.5),
        "peer_k1": nrm((L, PEER_HEADS, N_KEYS, PK_HALF), PK_HALF ** -0.5),
        "peer_k2": nrm((L, PEER_HEADS, N_KEYS, PK_HALF), PK_HALF ** -0.5),
        "peer_u": nrm((L, N_EXPERTS, D_MODEL), D_MODEL ** -0.5),
        "peer_v": nrm((L, N_EXPERTS, D_MODEL), BETA * PEER_HEADS ** -0.5),
        "ln2_g": 1.0 + nrm((L, D_MODEL), 0.02),
        "ln2_b": nrm((L, D_MODEL), 0.02),
        "ple_w": nrm((L, PLE_DIM, D_MODEL), BETA * PLE_DIM ** -0.5),
        "ple_gate_w": nrm((L, D_MODEL, D_MODEL), D_MODEL ** -0.5),
        "ln3_g": 1.0 + nrm((L, D_MODEL), 0.02),
        "ln3_b": nrm((L, D_MODEL), 0.02),
    }


def reference(x_prompt, x_sample, p_prompt, p_sample, cache_ckv, cache_krope, state_wkv, state_shift,
              page_table, w_in, kv_norm_g, w_uk, w_uv, rw_mu, rw_w0, rw_w_up, rw_a0, rw_a_up, rw_g_up,
              rw_k_k, rw_k_a, rw_r_k, rw_gn_g, rw_gn_b, w_o, ln1_g, ln1_b, peer_wq, peer_k1, peer_k2,
              peer_u, peer_v, ln2_g, ln2_b, ple_w, ple_gate_w, ln3_g, ln3_b):
    layer_params = (w_in, kv_norm_g, w_uk, w_uv, rw_mu, rw_w0, rw_w_up, rw_a0, rw_a_up, rw_g_up,
                    rw_k_k, rw_k_a, rw_r_k, rw_gn_g, rw_gn_b, w_o, ln1_g, ln1_b,
                    peer_wq, peer_k1, peer_k2, peer_u, peer_v, ln2_g, ln2_b,
                    ple_w, ple_gate_w, ln3_g, ln3_b)
    B = x_prompt.shape[0]
    past_len = page_table.shape[1] * cache_ckv.shape[2]
    pos_p = jnp.arange(x_prompt.shape[1])
    pos_s = past_len + jnp.arange(x_sample.shape[1])
    xp, xs = x_prompt, x_sample
    ckv_p, kr_p, wkv_p, sh_p = [], [], [], []
    ckv_s, kr_s, wkv_s, sh_s = [], [], [], []
    for i in range(DEPTH):
        lp = [w[i] for w in layer_params]
        wkv0 = jnp.zeros((B, RW_HEADS, RW_HEAD, RW_HEAD), jnp.float32)
        sh0 = jnp.zeros((B, RW_COLS), xp.dtype)
        xp, c1, k1, s1, h1 = decoder_layer(xp, p_prompt[i], pos_p, wkv0, sh0, mla_prompt, *lp)
        attend_s = functools.partial(mla_sample, cache_ckv=cache_ckv[i], cache_krope=cache_krope[i],
                                     page_table=page_table)
        xs, c2, k2, s2, h2 = decoder_layer(xs, p_sample[i], pos_s, state_wkv[i], state_shift[i], attend_s, *lp)
        ckv_p.append(c1); kr_p.append(k1); wkv_p.append(s1); sh_p.append(h1)
        ckv_s.append(c2); kr_s.append(k2); wkv_s.append(s2); sh_s.append(h2)
    return (xp, xs, jnp.stack(ckv_p), jnp.stack(kr_p), jnp.stack(wkv_p), jnp.stack(sh_p),
            jnp.stack(ckv_s), jnp.stack(kr_s), jnp.stack(wkv_s), jnp.stack(sh_s))
```

```python
import functools
import math

import jax
import jax.numpy as jnp
from jax import lax
from jax.experimental import pallas as pl
from jax.experimental.pallas import tpu as pltpu

F32 = jnp.float32
BF16 = jnp.bfloat16

D_MODEL = 2048
MLA_HEADS = 8
QK_NOPE = 128
QK_ROPE = 64
V_HEAD = 128
KV_RANK = 512
ROPE_THETA = 10000.0
RW_HEADS = 16
RW_HEAD = 64
RW_WIDTH = RW_HEADS * RW_HEAD
DECAY_LORA = 64
AAA_LORA = 64
GATE_LORA = 160
RW_COLS = 3 * RW_WIDTH + DECAY_LORA + AAA_LORA + GATE_LORA
Q_COLS = MLA_HEADS * (QK_NOPE + QK_ROPE)
MLA_COLS = Q_COLS + KV_RANK + QK_ROPE
PEER_HEADS = 8
N_KEYS = 128
PK_DIM = 256
PK_HALF = PK_DIM // 2
PK_TOPK = 16
KEY_BITS = 7
TOPK_BITS = 4
PLE_DIM = 256
DEPTH = 1
ALPHA = (2 * DEPTH) ** 0.25
LN_EPS = 1e-5
RMS_EPS = 1e-6
GN_EPS = 64e-5
ATTN_SCALE = (QK_NOPE + QK_ROPE) ** -0.5

LANES = 128
RW_PAIRS = RW_HEADS // 2
VMEM_LIMIT = 56 * 1024 * 1024

NT_DIMS = (((1,), (1,)), ((), ()))


def _cparams(sem, vmem=VMEM_LIMIT):
    return pltpu.CompilerParams(dimension_semantics=sem, vmem_limit_bytes=vmem)


def _resident(shape):
    nd = len(shape)
    return pl.BlockSpec(shape, lambda *_: (0,) * nd, pipeline_mode=pl.Buffered(1))


def _nt(a, b):
    return lax.dot_general(a, b, NT_DIMS, preferred_element_type=F32)


def _mm(a, b):
    return jnp.dot(a, b, preferred_element_type=F32)


def _dot01(x, e):
    h = x.astype(BF16)
    r = x - h.astype(F32)
    m = r.astype(BF16)
    l = (r - m.astype(F32)).astype(BF16)
    return _mm(h, e) + _mm(m, e) + _mm(l, e)


def _head_sum(x, e, et):
    return _dot01(_dot01(x, e), et)


def _sigmoid(x):
    return 1.0 / (1.0 + jnp.exp(-x))


def _layer_norm(x, g, b):
    mu = jnp.mean(x, axis=-1, keepdims=True)
    d = x - mu
    var = jnp.mean(d * d, axis=-1, keepdims=True)
    return d * lax.rsqrt(var + LN_EPS) * g + b


def _mla_proj_common(x_ref, w_ref, g_ref, cq_ref, sq_ref, ck_ref, sk_ref):
    x = x_ref[...].astype(BF16)
    res = _mm(x, w_ref[...])
    nq = MLA_HEADS * QK_NOPE
    rq = MLA_HEADS * QK_ROPE
    qn = res[:, :nq]
    qr = res[:, nq:nq + rq] * cq_ref[...] + res[:, nq + rq:nq + 2 * rq] * sq_ref[...]
    o = nq + 2 * rq
    c = res[:, o:o + KV_RANK]
    ckv = c * lax.rsqrt(jnp.mean(c * c, axis=-1, keepdims=True) + RMS_EPS) * g_ref[...]
    o += KV_RANK
    kr = res[:, o:o + QK_ROPE] * ck_ref[...] + res[:, o + QK_ROPE:o + 2 * QK_ROPE] * sk_ref[...]
    return qn, qr, ckv, kr


def _mla_proj_prompt_kernel(x_ref, w_ref, g_ref, cq_ref, sq_ref, ck_ref, sk_ref, wuk_ref, wuv_ref,
                            qn_o, qr_o, ckv_o, kr_o, kn_o, v_o):
    qn, qr, ckv, kr = _mla_proj_common(x_ref, w_ref, g_ref, cq_ref, sq_ref, ck_ref, sk_ref)
    qn_o[...] = qn.astype(BF16)
    qr_o[...] = qr.astype(BF16)
    ckv_o[...] = ckv
    kr_o[...] = kr
    cb = ckv.astype(BF16)
    kn_o[...] = _mm(cb, wuk_ref[...]).astype(BF16)
    v_o[...] = _mm(cb, wuv_ref[...]).astype(BF16)


def _mla_proj_sample_kernel(x_ref, w_ref, g_ref, cq_ref, sq_ref, ck_ref, sk_ref, wukt_ref,
                            ql_o, qr_o, ckv_o, kr_o):
    qn, qr, ckv, kr = _mla_proj_common(x_ref, w_ref, g_ref, cq_ref, sq_ref, ck_ref, sk_ref)
    qr_o[...] = qr.astype(BF16)
    ckv_o[...] = ckv
    kr_o[...] = kr
    for h in range(MLA_HEADS):
        qh = qn[:, h * QK_NOPE:(h + 1) * QK_NOPE].astype(BF16)
        ql_o[:, h * KV_RANK:(h + 1) * KV_RANK] = _mm(qh, wukt_ref[h]).astype(BF16)


def _mla_proj(x2d, w_all, kv_g, cq, sq, ck, sk, up_weights, prompt, tm):
    n = x2d.shape[0]
    wcols = w_all.shape[1]
    nq, rq = MLA_HEADS * QK_NOPE, MLA_HEADS * QK_ROPE
    row = lambda c: pl.BlockSpec((tm, c), lambda i: (i, 0))
    in_specs = [row(D_MODEL), _resident((D_MODEL, wcols)), _resident((1, KV_RANK)),
                row(rq), row(rq), row(QK_ROPE), row(QK_ROPE)]
    if prompt:
        kern = _mla_proj_prompt_kernel
        in_specs += [_resident((KV_RANK, nq)), _resident((KV_RANK, nq))]
        out_shape = [jax.ShapeDtypeStruct((n, nq), BF16), jax.ShapeDtypeStruct((n, rq), BF16),
                     jax.ShapeDtypeStruct((n, KV_RANK), F32), jax.ShapeDtypeStruct((n, QK_ROPE), F32),
                     jax.ShapeDtypeStruct((n, nq), BF16), jax.ShapeDtypeStruct((n, nq), BF16)]
        out_specs = [row(nq), row(rq), row(KV_RANK), row(QK_ROPE), row(nq), row(nq)]
    else:
        kern = _mla_proj_sample_kernel
        in_specs += [_resident((MLA_HEADS, QK_NOPE, KV_RANK))]
        out_shape = [jax.ShapeDtypeStruct((n, MLA_HEADS * KV_RANK), BF16), jax.ShapeDtypeStruct((n, rq), BF16),
                     jax.ShapeDtypeStruct((n, KV_RANK), F32), jax.ShapeDtypeStruct((n, QK_ROPE), F32)]
        out_specs = [row(MLA_HEADS * KV_RANK), row(rq), row(KV_RANK), row(QK_ROPE)]
    return pl.pallas_call(
        kern, grid=(n // tm,), in_specs=in_specs, out_specs=out_specs, out_shape=out_shape,
        compiler_params=_cparams(("parallel",)), name="mla_proj",
    )(x2d, w_all, kv_g, cq, sq, ck, sk, *up_weights)


def _flash_kernel(qn_ref, qr_ref, kn_ref, kr_ref, v_ref, o_ref, m_ref, l_ref, acc_ref, *, tq, tk, nk):
    i = pl.program_id(1)
    j = pl.program_id(2)

    @pl.when(j == 0)
    def _():
        m_ref[...] = jnp.full(m_ref.shape, -jnp.inf, F32)
        l_ref[...] = jnp.zeros(l_ref.shape, F32)
        acc_ref[...] = jnp.zeros(acc_ref.shape, F32)

    @pl.when(j * tk <= i * tq + tq - 1)
    def _():
        krb = kr_ref[0].astype(BF16)
        qpos = i * tq + lax.broadcasted_iota(jnp.int32, (tq, tk), 0)
        kpos = j * tk + lax.broadcasted_iota(jnp.int32, (tq, tk), 1)
        visible = kpos <= qpos
        for h in range(MLA_HEADS):
            ns = slice(h * QK_NOPE, (h + 1) * QK_NOPE)
            s = _nt(qn_ref[0, :, ns], kn_ref[0, :, ns])
            s = s + _nt(qr_ref[0, :, h * QK_ROPE:(h + 1) * QK_ROPE], krb)
            s = jnp.where(visible, s * ATTN_SCALE, -jnp.inf)
            m_prev = m_ref[h]
            m_new = jnp.maximum(m_prev, jnp.max(s, axis=1, keepdims=True))
            alpha = jnp.exp(m_prev - m_new)
            p = jnp.exp(s - m_new)
            l_ref[h] = alpha * l_ref[h] + jnp.sum(p, axis=1, keepdims=True)
            acc_ref[h] = alpha * acc_ref[h] + _mm(p.astype(BF16), v_ref[0, :, h * V_HEAD:(h + 1) * V_HEAD])
            m_ref[h] = m_new

    @pl.when(j == nk - 1)
    def _():
        for h in range(MLA_HEADS):
            o_ref[0, :, h * V_HEAD:(h + 1) * V_HEAD] = (acc_ref[h] / l_ref[h]).astype(o_ref.dtype)


def _mla_prompt_attention(qn, qr, kn, kr, v, tq, tk):
    b, s, _ = qn.shape
    nq, nk = s // tq, s // tk
    last = lambda i: (i * tq + tq - 1) // tk
    qspec = lambda c: pl.BlockSpec((1, tq, c), lambda bb, i, j: (bb, i, 0))
    kspec = lambda c: pl.BlockSpec((1, tk, c), lambda bb, i, j: (bb, jnp.minimum(j, last(i)), 0))
    return pl.pallas_call(
        functools.partial(_flash_kernel, tq=tq, tk=tk, nk=nk),
        grid=(b, nq, nk),
        in_specs=[qspec(qn.shape[2]), qspec(qr.shape[2]), kspec(kn.shape[2]), kspec(kr.shape[2]), kspec(v.shape[2])],
        out_specs=qspec(v.shape[2]),
        out_shape=jax.ShapeDtypeStruct(v.shape, BF16),
        scratch_shapes=[pltpu.VMEM((MLA_HEADS, tq, 1), F32), pltpu.VMEM((MLA_HEADS, tq, 1), F32),
                        pltpu.VMEM((MLA_HEADS, tq, V_HEAD), F32)],
        compiler_params=_cparams(("parallel", "parallel", "arbitrary")), name="mla_prompt_attention",
    )(qn, qr, kn, kr, v)


def _decode_kernel(pt_ref, ql_ref, qr_ref, cn_ref, kn_ref, *rest, pages, ng):
    del pt_ref
    c_refs, k_refs = rest[:pages], rest[pages:2 * pages]
    o_ref, m_ref, l_ref, acc_ref = rest[2 * pages:]
    g = pl.program_id(1)

    @pl.when(g == 0)
    def _():
        m_ref[...] = jnp.full(m_ref.shape, -jnp.inf, F32)
        l_ref[...] = jnp.zeros(l_ref.shape, F32)
        acc_ref[...] = jnp.zeros(acc_ref.shape, F32)

    ql = ql_ref[0]
    qr = qr_ref[0]
    cs, ss = [], []
    for i in range(pages):
        c = c_refs[i][...].astype(BF16)
        k = k_refs[i][...].astype(BF16)
        cs.append(c)
        ss.append(_nt(ql, c) + _nt(qr, k))
    s = jnp.concatenate(ss, axis=1) * ATTN_SCALE
    m_prev = m_ref[...]
    m_new = jnp.maximum(m_prev, jnp.max(s, axis=1, keepdims=True))
    alpha = jnp.exp(m_prev - m_new)
    p = jnp.exp(s - m_new)
    l_new = alpha * l_ref[...] + jnp.sum(p, axis=1, keepdims=True)
    pb = p.astype(BF16)
    page = cs[0].shape[0]
    pv = _mm(pb[:, :page], cs[0])
    for i in range(1, pages):
        pv = pv + _mm(pb[:, i * page:(i + 1) * page], cs[i])
    acc_new = alpha * acc_ref[...] + pv
    m_ref[...] = m_new
    l_ref[...] = l_new
    acc_ref[...] = acc_new

    @pl.when(g == ng - 1)
    def _():
        cn = cn_ref[0].astype(BF16).astype(F32)
        kn = kn_ref[0].astype(BF16).astype(F32)
        s_self = (jnp.sum(ql.astype(F32) * cn, axis=1, keepdims=True)
                  + jnp.sum(qr.astype(F32) * kn, axis=1, keepdims=True)) * ATTN_SCALE
        m_f = jnp.maximum(m_new, s_self)
        a_f = jnp.exp(m_new - m_f)
        p_self = jnp.exp(s_self - m_f)
        l_f = a_f * l_new + p_self
        acc_f = a_f * acc_new + p_self.astype(BF16).astype(F32) * cn
        o_ref[0] = acc_f / l_f


def _mla_sample_attention(page_table, q_lat, q_rope, c_new, kr_new, cache_ckv, cache_krope, pages):
    db, n_pages = page_table.shape
    _, page, _ = cache_ckv.shape
    ng = n_pages // pages
    pt = page_table.reshape(-1)
    per_seq = lambda shape: pl.BlockSpec((1,) + shape, lambda b, g, pt_: (b, 0, 0))

    def page_spec(width, i):
        return pl.BlockSpec((None, page, width),
                            lambda b, g, pt_: (pt_[b * n_pages + g * pages + i], 0, 0))

    grid_spec = pltpu.PrefetchScalarGridSpec(
        num_scalar_prefetch=1, grid=(db, ng),
        in_specs=[per_seq((MLA_HEADS, KV_RANK)), per_seq((MLA_HEADS, QK_ROPE)),
                  per_seq((1, KV_RANK)), per_seq((1, QK_ROPE))]
                 + [page_spec(KV_RANK, i) for i in range(pages)]
                 + [page_spec(QK_ROPE, i) for i in range(pages)],
        out_specs=per_seq((MLA_HEADS, KV_RANK)),
        scratch_shapes=[pltpu.VMEM((MLA_HEADS, 1), F32), pltpu.VMEM((MLA_HEADS, 1), F32),
                        pltpu.VMEM((MLA_HEADS, KV_RANK), F32)])
    return pl.pallas_call(
        functools.partial(_decode_kernel, pages=pages, ng=ng),
        grid_spec=grid_spec,
        out_shape=jax.ShapeDtypeStruct((db, MLA_HEADS, KV_RANK), F32),
        compiler_params=_cparams(("parallel", "arbitrary")), name="mla_sample_attention",
    )(pt, q_lat, q_rope, c_new, kr_new, *([cache_ckv] * pages), *([cache_krope] * pages))


def _head_up_kernel(o_ref, w_ref, out_ref):
    out_ref[...] = _mm(o_ref[...].astype(BF16), w_ref[...]).astype(out_ref.dtype)


def _head_up(o_lat2d, w_uv2):
    n = o_lat2d.shape[0]
    return pl.pallas_call(
        _head_up_kernel, grid=(MLA_HEADS,),
        in_specs=[pl.BlockSpec((n, KV_RANK), lambda h: (0, h)), pl.BlockSpec((KV_RANK, V_HEAD), lambda h: (0, h))],
        out_specs=pl.BlockSpec((n, V_HEAD), lambda h: (0, h)),
        out_shape=jax.ShapeDtypeStruct((n, MLA_HEADS * V_HEAD), BF16),
        compiler_params=_cparams(("parallel",)), name="mla_head_up",
    )(o_lat2d, w_uv2)


def _rwkv_terms(p, prev, mu_ref, w0_ref, wup_ref, a0_ref, aup_ref, gup_ref, kk_ref, ka_ref, rk_ref,
                e_ref, et_ref, outs):
    r_o, w_o, k_o, v_o, n_o, b_o, g_o, bonus_o = outs
    xs = p + mu_ref[...] * (prev - p)
    w3 = 3 * RW_WIDTH
    r = xs[:, :RW_WIDTH]
    k0 = xs[:, RW_WIDTH:2 * RW_WIDTH]
    v = xs[:, 2 * RW_WIDTH:w3]
    xw = xs[:, w3:w3 + DECAY_LORA]
    xa = xs[:, w3 + DECAY_LORA:w3 + DECAY_LORA + AAA_LORA]
    xg = xs[:, w3 + DECAY_LORA + AAA_LORA:]
    y = w0_ref[...] + _mm(jnp.tanh(xw).astype(BF16), wup_ref[...])
    decay = jnp.exp(-math.exp(-0.5) * _sigmoid(y))
    a = _sigmoid(a0_ref[...] + _mm(xa.astype(BF16), aup_ref[...]))
    g = _mm(_sigmoid(xg).astype(BF16), gup_ref[...])
    kk = k0 * kk_ref[...]
    e, et = e_ref[...], et_ref[...]
    kk = kk * lax.rsqrt(jnp.maximum(_head_sum(kk * kk, e, et), 1e-24))
    k = k0 * (1.0 + (a - 1.0) * ka_ref[...])
    r_o[...] = r
    w_o[...] = decay
    k_o[...] = k
    v_o[...] = v
    n_o[...] = -kk
    b_o[...] = kk * a
    g_o[...] = g
    bonus_o[...] = _head_sum(r * k * rk_ref[...], e, et) * v


def _rwkv_proj_prompt_kernel(x_ref, w_ref, mu_ref, w0_ref, wup_ref, a0_ref, aup_ref, gup_ref,
                             kk_ref, ka_ref, rk_ref, e_ref, et_ref,
                             r_o, w_o, k_o, v_o, n_o, b_o, g_o, bonus_o, last_o, carry_ref):
    tm = x_ref.shape[1]

    @pl.when(pl.program_id(1) == 0)
    def _():
        carry_ref[...] = jnp.zeros(carry_ref.shape, F32)

    p = _mm(x_ref[0].astype(BF16), w_ref[...])
    rows = lax.broadcasted_iota(jnp.int32, p.shape, 0)
    prev = jnp.where(rows == 0, carry_ref[...], pltpu.roll(p, 1, axis=0))
    last = p[tm - 1:tm, :]
    carry_ref[...] = last
    last_o[0] = last
    outs = tuple(o.at[0] for o in (r_o, w_o, k_o, v_o, n_o, b_o, g_o, bonus_o))
    _rwkv_terms(p, prev, mu_ref, w0_ref, wup_ref, a0_ref, aup_ref, gup_ref, kk_ref, ka_ref, rk_ref,
                e_ref, et_ref, outs)


def _rwkv_proj_sample_kernel(x_ref, prev_ref, w_ref, mu_ref, w0_ref, wup_ref, a0_ref, aup_ref, gup_ref,
                             kk_ref, ka_ref, rk_ref, e_ref, et_ref,
                             r_o, w_o, k_o, v_o, n_o, b_o, g_o, bonus_o, last_o):
    p = _mm(x_ref[...].astype(BF16), w_ref[...])
    last_o[...] = p
    _rwkv_terms(p, prev_ref[...], mu_ref, w0_ref, wup_ref, a0_ref, aup_ref, gup_ref, kk_ref, ka_ref, rk_ref,
                e_ref, et_ref, (r_o, w_o, k_o, v_o, n_o, b_o, g_o, bonus_o))


def _rwkv_weight_specs():
    vec = lambda c: _resident((1, c))
    return [_resident((D_MODEL, RW_COLS)), vec(RW_COLS), vec(RW_WIDTH), _resident((DECAY_LORA, RW_WIDTH)),
            vec(RW_WIDTH), _resident((AAA_LORA, RW_WIDTH)), _resident((GATE_LORA, RW_WIDTH)),
            vec(RW_WIDTH), vec(RW_WIDTH), vec(RW_WIDTH),
            _resident((RW_WIDTH, LANES)), _resident((LANES, RW_WIDTH))]


def _rwkv_proj_prompt(x, rw_weights, tm):
    b, t, _ = x.shape
    tile = lambda c: pl.BlockSpec((1, tm, c), lambda bb, i: (bb, i, 0))
    wide = jax.ShapeDtypeStruct((b, t, RW_WIDTH), F32)
    return pl.pallas_call(
        _rwkv_proj_prompt_kernel, grid=(b, t // tm),
        in_specs=[tile(D_MODEL)] + _rwkv_weight_specs(),
        out_specs=[tile(RW_WIDTH)] * 8 + [pl.BlockSpec((1, 1, RW_COLS), lambda bb, i: (bb, 0, 0))],
        out_shape=[wide] * 8 + [jax.ShapeDtypeStruct((b, 1, RW_COLS), F32)],
        scratch_shapes=[pltpu.VMEM((1, RW_COLS), F32)],
        compiler_params=_cparams(("parallel", "arbitrary")), name="rwkv_proj_prompt",
    )(x, *rw_weights)


def _rwkv_proj_sample(x2d, prev, rw_weights):
    n = x2d.shape[0]
    full = lambda c: pl.BlockSpec((n, c), lambda i: (0, 0))
    wide = jax.ShapeDtypeStruct((n, RW_WIDTH), F32)
    return pl.pallas_call(
        _rwkv_proj_sample_kernel, grid=(1,),
        in_specs=[full(D_MODEL), full(RW_COLS)] + _rwkv_weight_specs(),
        out_specs=[full(RW_WIDTH)] * 8 + [full(RW_COLS)],
        out_shape=[wide] * 8 + [jax.ShapeDtypeStruct((n, RW_COLS), F32)],
        compiler_params=_cparams(("arbitrary",)), name="rwkv_proj_sample",
    )(x2d, prev, *rw_weights)


def _rwkv_scan_kernel(s0_ref, r_ref, w_ref, k_ref, v_ref, n_ref, b_ref, out_ref, st_ref, state_ref, *, nb, tc, nc):
    c = pl.program_id(1)

    @pl.when(c == 0)
    def _():
        for b in range(nb):
            for p in range(RW_PAIRS):
                state_ref[b, p] = jnp.concatenate([s0_ref[b, 2 * p], s0_ref[b, 2 * p + 1]], axis=1)

    lane = lax.broadcasted_iota(jnp.int32, (RW_HEAD, LANES), 1)
    sub = lax.broadcasted_iota(jnp.int32, (RW_HEAD, LANES), 0)
    first = lane < RW_HEAD
    diag = (lane & (RW_HEAD - 1)) == sub

    def head_sum(x):
        lo = jnp.sum(jnp.where(first, x, 0.0), axis=1, keepdims=True)
        hi = jnp.sum(jnp.where(first, 0.0, x), axis=1, keepdims=True)
        return jnp.where(first, lo, hi)

    def advance(s, r, w, k, v, n, bb):
        sa = head_sum(s * n)
        v_col = head_sum(jnp.where(diag, v, 0.0))
        s = s * w + sa * bb + v_col * k
        o_col = head_sum(s * r)
        return s, jnp.sum(jnp.where(diag, o_col, 0.0), axis=0, keepdims=True)

    ins = (r_ref, w_ref, k_ref, v_ref, n_ref, b_ref)
    group = 8 if tc % 8 == 0 else 1

    def steps(t0):
        for b in range(nb):
            for p in range(RW_PAIRS):
                cols = slice(p * LANES, (p + 1) * LANES)
                rows = [ref[b, pl.ds(t0, group), cols] for ref in ins]
                s = state_ref[b, p]
                outs = []
                for i in range(group):
                    s, o = advance(s, *(x[i:i + 1, :] for x in rows))
                    outs.append(o)
                state_ref[b, p] = s
                out_ref[b, pl.ds(t0, group), cols] = outs[0] if group == 1 else jnp.concatenate(outs, axis=0)

    if tc == group:
        steps(0)
    else:
        def body(tg, carry):
            steps(pl.multiple_of(tg * group, group))
            return carry
        lax.fori_loop(0, tc // group, body, 0)

    @pl.when(c == nc - 1)
    def _():
        for b in range(nb):
            for p in range(RW_PAIRS):
                s = state_ref[b, p]
                st_ref[b, 2 * p] = s[:, :RW_HEAD]
                st_ref[b, 2 * p + 1] = s[:, RW_HEAD:]


def _rwkv_scan(s0, r, w, k, v, n, bb, nb, tc):
    b, t, _ = r.shape
    nc = t // tc
    seq = pl.BlockSpec((nb, tc, RW_WIDTH), lambda i, c: (i, c, 0))
    st = pl.BlockSpec((nb, RW_HEADS, RW_HEAD, RW_HEAD), lambda i, c: (i, 0, 0, 0))
    return pl.pallas_call(
        functools.partial(_rwkv_scan_kernel, nb=nb, tc=tc, nc=nc),
        grid=(b // nb, nc),
        in_specs=[st] + [seq] * 6,
        out_specs=[seq, st],
        out_shape=[jax.ShapeDtypeStruct((b, t, RW_WIDTH), F32),
                   jax.ShapeDtypeStruct((b, RW_HEADS, RW_HEAD, RW_HEAD), F32)],
        scratch_shapes=[pltpu.VMEM((nb, RW_PAIRS, RW_HEAD, LANES), F32)],
        compiler_params=_cparams(("parallel", "arbitrary")), name="rwkv_scan",
    )(s0, r, w, k, v, n, bb)


def _mix_out_kernel(x_ref, om_ref, raw_ref, g_ref, bonus_ref, gng_ref, gnb_ref, e_ref, et_ref,
                    wo1_ref, wo2_ref, lng_ref, lnb_ref, o_ref):
    e, et = e_ref[...], et_ref[...]
    raw = raw_ref[...]
    mu = _head_sum(raw, e, et) * (1.0 / RW_HEAD)
    d = raw - mu
    var = _head_sum(d * d, e, et) * (1.0 / RW_HEAD)
    o_rw = (d * lax.rsqrt(var + GN_EPS) * gng_ref[...] + gnb_ref[...] + bonus_ref[...]) * g_ref[...]
    mixed = _mm(om_ref[...], wo1_ref[...]) + _mm(o_rw.astype(BF16), wo2_ref[...])
    o_ref[...] = _layer_norm(ALPHA * x_ref[...] + mixed, lng_ref[...], lnb_ref[...])


def _mix_out(x2d, o_mla, raw, g, bonus, gng, gnb, e, et, wo1, wo2, lng, lnb, tm):
    n = x2d.shape[0]
    row = lambda c: pl.BlockSpec((tm, c), lambda i: (i, 0))
    vec = lambda c: _resident((1, c))
    return pl.pallas_call(
        _mix_out_kernel, grid=(n // tm,),
        in_specs=[row(D_MODEL), row(RW_WIDTH), row(RW_WIDTH), row(RW_WIDTH), row(RW_WIDTH),
                  vec(RW_WIDTH), vec(RW_WIDTH), _resident((RW_WIDTH, LANES)), _resident((LANES, RW_WIDTH)),
                  _resident(wo1.shape), _resident(wo2.shape), vec(D_MODEL), vec(D_MODEL)],
        out_specs=row(D_MODEL),
        out_shape=jax.ShapeDtypeStruct((n, D_MODEL), F32),
        compiler_params=_cparams(("parallel",)), name="mix_out_ln1",
    )(x2d, o_mla, raw, g, bonus, gng, gnb, e, et, wo1, wo2, lng, lnb)


def _peer_route_kernel(x_ref, wq_ref, k1_ref, k2_ref, i1_o, i2_o, g_o):
    tm = x_ref.shape[0]
    kk2 = PK_TOPK * PK_TOPK
    q = _mm(x_ref[...].astype(BF16), wq_ref[...])
    lane_k = lax.broadcasted_iota(jnp.int32, (tm, N_KEYS), 1).astype(F32)
    lane_ci = lax.broadcasted_iota(jnp.int32, (tm, kk2), 1)
    lane_c = lane_ci.astype(F32)
    slot = lax.broadcasted_iota(jnp.int32, (tm, PEER_HEADS * PK_TOPK), 1)
    neg = -jnp.inf

    def top_keys(scores, spread):
        def body(r, carry):
            s, vals, idxs = carry
            m = jnp.max(s, axis=1, keepdims=True)
            idx = jnp.min(jnp.where(s == m, lane_k, float(N_KEYS)), axis=1, keepdims=True)
            s = jnp.where(lane_k == idx, neg, s)
            hit = spread == r
            return s, jnp.where(hit, m, vals), jnp.where(hit, idx, idxs)
        zero = jnp.zeros((tm, kk2), F32)
        _, vals, idxs = lax.fori_loop(0, PK_TOPK, body, (scores, zero, zero))
        return vals, idxs

    e_acc = jnp.zeros((tm, PEER_HEADS * PK_TOPK), F32)
    p_acc = jnp.zeros((tm, PEER_HEADS * PK_TOPK), F32)
    d_acc = jnp.ones((tm, PEER_HEADS * PK_TOPK), F32)
    for h in range(PEER_HEADS):
        q1 = q[:, h * PK_DIM:h * PK_DIM + PK_HALF].astype(BF16)
        q2 = q[:, h * PK_DIM + PK_HALF:(h + 1) * PK_DIM].astype(BF16)
        v1, i1 = top_keys(_nt(q1, k1_ref[h]), lane_ci >> TOPK_BITS)
        v2, i2 = top_keys(_nt(q2, k2_ref[h]), lane_ci & (PK_TOPK - 1))
        cand = v1 + v2
        expert = i1 * float(N_KEYS) + i2

        def body(r, carry, cand=cand, expert=expert, h=h):
            cnd, top, den, e_a, p_a = carry
            m = jnp.max(cnd, axis=1, keepdims=True)
            idx = jnp.min(jnp.where(cnd == m, lane_c, float(kk2)), axis=1, keepdims=True)
            hit = lane_c == idx
            e = jnp.max(jnp.where(hit, expert, -1.0), axis=1, keepdims=True)
            cnd = jnp.where(hit, neg, cnd)
            top = jnp.where(r == 0, m, top)
            pr = jnp.exp(m - top)
            col = slot == h * PK_TOPK + r
            return cnd, top, den + pr, jnp.where(col, e, e_a), jnp.where(col, pr, p_a)

        zcol = jnp.zeros((tm, 1), F32)
        _, _, den, e_acc, p_acc = lax.fori_loop(0, PK_TOPK, body, (cand, zcol, zcol, e_acc, p_acc))
        d_acc = jnp.where((slot >> TOPK_BITS) == h, den, d_acc)
    e_int = e_acc.astype(jnp.int32)
    i1_o[...] = e_int >> KEY_BITS
    i2_o[...] = e_int & (N_KEYS - 1)
    g_o[...] = p_acc / d_acc


def _peer_route(x2d, wq, k1, k2, tm):
    n = x2d.shape[0]
    slots = PEER_HEADS * PK_TOPK
    row = lambda c: pl.BlockSpec((tm, c), lambda i: (i, 0))
    return pl.pallas_call(
        _peer_route_kernel, grid=(n // tm,),
        in_specs=[row(D_MODEL), _resident(wq.shape), _resident(k1.shape), _resident(k2.shape)],
        out_specs=[row(slots)] * 3,
        out_shape=[jax.ShapeDtypeStruct((n, slots), jnp.int32), jax.ShapeDtypeStruct((n, slots), jnp.int32),
                   jax.ShapeDtypeStruct((n, slots), F32)],
        compiler_params=_cparams(("parallel",)), name="peer_route",
    )(x2d, wq, k1, k2)


def _peer_weights_kernel(i1_ref, i2_ref, g_ref, wt_ref):
    tm = i1_ref.shape[0]
    key = lax.broadcasted_iota(jnp.int32, (N_KEYS, PEER_HEADS * PK_TOPK), 0)

    def body(t, carry):
        i1 = i1_ref[pl.ds(t, 1), :]
        i2 = i2_ref[pl.ds(t, 1), :]
        g = g_ref[pl.ds(t, 1), :]
        g1 = jnp.where(key == i1, g, 0.0)
        hi = g1.astype(BF16)
        lo = (g1 - hi.astype(F32)).astype(BF16)
        g2 = jnp.where(key == i2, 1.0, 0.0).astype(BF16)
        wt_ref[t] = _nt(jnp.concatenate([hi, lo], axis=1), jnp.concatenate([g2, g2], axis=1))
        return carry

    lax.fori_loop(0, tm, body, 0)


def _peer_weights(i1, i2, g, tm):
    n, slots = i1.shape
    row = pl.BlockSpec((tm, slots), lambda i: (i, 0))
    return pl.pallas_call(
        _peer_weights_kernel, grid=(n // tm,),
        in_specs=[row, row, row],
        out_specs=pl.BlockSpec((tm, N_KEYS, N_KEYS), lambda i: (i, 0, 0)),
        out_shape=jax.ShapeDtypeStruct((n, N_KEYS, N_KEYS), F32),
        compiler_params=_cparams(("parallel",)), name="peer_weights",
    )(i1, i2, g)


def _peer_dense_kernel(x_ref, wt_ref, u_ref, v_ref, y_ref, xb_ref, z_ref, *, rows_per_step):
    c = pl.program_id(1)
    tt = x_ref.shape[0]

    @pl.when(c == 0)
    def _():
        xb_ref[...] = x_ref[...].astype(BF16)
        y_ref[...] = jnp.zeros(y_ref.shape, F32)

    hid = _nt(xb_ref[...], u_ref[...])
    for a in range(rows_per_step):
        h = hid[:, a * N_KEYS:(a + 1) * N_KEYS]
        w = wt_ref[:, a, :]
        gelu = 0.5 * h * (1.0 + lax.erf(h * math.sqrt(0.5)))
        z_ref[:, a * N_KEYS:(a + 1) * N_KEYS] = (w * gelu).astype(BF16)
    y_ref[...] += _mm(z_ref[...], v_ref[...])


def _peer_dense(x2d, wt, u, v, tt, rows_per_step):
    n = x2d.shape[0]
    ne = rows_per_step * N_KEYS
    return pl.pallas_call(
        functools.partial(_peer_dense_kernel, rows_per_step=rows_per_step),
        grid=(n // tt, N_KEYS // rows_per_step),
        in_specs=[pl.BlockSpec((tt, D_MODEL), lambda i, c: (i, 0)),
                  pl.BlockSpec((tt, rows_per_step, N_KEYS), lambda i, c: (i, c, 0)),
                  pl.BlockSpec((ne, D_MODEL), lambda i, c: (c, 0)),
                  pl.BlockSpec((ne, D_MODEL), lambda i, c: (c, 0))],
        out_specs=pl.BlockSpec((tt, D_MODEL), lambda i, c: (i, 0)),
        out_shape=jax.ShapeDtypeStruct((n, D_MODEL), F32),
        scratch_shapes=[pltpu.VMEM((tt, D_MODEL), BF16), pltpu.VMEM((tt, ne), BF16)],
        compiler_params=_cparams(("parallel", "arbitrary")), name="peer_dense",
    )(x2d, wt, u, v)


def _ple_kernel(x1_ref, y_ref, p_ref, l2g_ref, l2b_ref, wg_ref, wp_ref, l3g_ref, l3b_ref, o_ref):
    x2 = _layer_norm(ALPHA * x1_ref[...] + y_ref[...], l2g_ref[...], l2b_ref[...])
    gate = _sigmoid(_mm(x2.astype(BF16), wg_ref[...]))
    ple = _mm(p_ref[...].astype(BF16), wp_ref[...]) * gate
    o_ref[...] = _layer_norm(ALPHA * x2 + ple, l3g_ref[...], l3b_ref[...])


def _ple(x1, y, p_emb, l2g, l2b, wg, wp, l3g, l3b, tm):
    n = x1.shape[0]
    row = lambda c: pl.BlockSpec((tm, c), lambda i: (i, 0))
    vec = _resident((1, D_MODEL))
    return pl.pallas_call(
        _ple_kernel, grid=(n // tm,),
        in_specs=[row(D_MODEL), row(D_MODEL), row(PLE_DIM), vec, vec, _resident(wg.shape), _resident(wp.shape), vec, vec],
        out_specs=row(D_MODEL),
        out_shape=jax.ShapeDtypeStruct((n, D_MODEL), F32),
        compiler_params=_cparams(("parallel",)), name="ple_ln2_ln3",
    )(x1, y, p_emb, l2g, l2b, wg, wp, l3g, l3b)


def _rope_tables(pos):
    inv = ROPE_THETA ** (-jnp.arange(0, QK_ROPE, 2, dtype=F32) / QK_ROPE)
    ang = pos.astype(F32)[:, None] * inv[None, :]
    cos, sin = jnp.cos(ang), jnp.sin(ang)
    ck = jnp.concatenate([cos, cos], axis=-1)
    sk = jnp.concatenate([-sin, sin], axis=-1)
    return jnp.tile(ck, (1, MLA_HEADS)), jnp.tile(sk, (1, MLA_HEADS)), ck, sk


def _prepare_weights(w_in, kv_norm_g, w_uk, w_uv, rw_mu, rw_w0, rw_w_up, rw_a0, rw_a_up, rw_g_up,
                     rw_k_k, rw_k_a, rw_r_k, rw_gn_g, rw_gn_b, w_o, ln1_g, ln1_b,
                     peer_wq, peer_k1, peer_k2, peer_u, peer_v, ln2_g, ln2_b,
                     ple_w, ple_gate_w, ln3_g, ln3_b):
    half = QK_ROPE // 2
    swap = lambda w: jnp.concatenate([w[..., half:], w[..., :half]], axis=-1)
    wq = w_in[:, :Q_COLS].reshape(D_MODEL, MLA_HEADS, QK_NOPE + QK_ROPE)
    wq_n = wq[:, :, :QK_NOPE].reshape(D_MODEL, -1)
    wq_r = wq[:, :, QK_NOPE:]
    w_c = w_in[:, Q_COLS:Q_COLS + KV_RANK]
    w_kr = w_in[:, Q_COLS + KV_RANK:MLA_COLS]
    w_mla = jnp.concatenate([wq_n, wq_r.reshape(D_MODEL, -1), swap(wq_r).reshape(D_MODEL, -1),
                             w_c, w_kr, swap(w_kr)], axis=1).astype(BF16)
    row = lambda a: a.reshape(1, -1)
    head_of_lane = jnp.arange(RW_WIDTH) // RW_HEAD
    e = (head_of_lane[:, None] == jnp.arange(LANES)[None, :]).astype(BF16)
    return dict(
        w_mla=w_mla, kv_g=row(kv_norm_g),
        w_uk2=w_uk.reshape(KV_RANK, -1).astype(BF16), w_uv2=w_uv.reshape(KV_RANK, -1).astype(BF16),
        w_ukt=jnp.transpose(w_uk, (1, 2, 0)).astype(BF16),
        rw=(w_in[:, MLA_COLS:].astype(BF16), row(rw_mu), row(rw_w0), rw_w_up.astype(BF16), row(rw_a0),
            rw_a_up.astype(BF16), rw_g_up.astype(BF16), row(rw_k_k), row(rw_k_a), row(rw_r_k), e, e.T),
        gn_g=row(rw_gn_g), gn_b=row(rw_gn_b), e=e, et=e.T,
        wo1=w_o[:MLA_HEADS * V_HEAD].astype(BF16), wo2=w_o[MLA_HEADS * V_HEAD:].astype(BF16),
        ln1_g=row(ln1_g), ln1_b=row(ln1_b),
        peer_wq=peer_wq.astype(BF16), peer_k1=peer_k1.astype(BF16), peer_k2=peer_k2.astype(BF16),
        peer_u=peer_u.astype(BF16), peer_v=peer_v.astype(BF16),
        ln2_g=row(ln2_g), ln2_b=row(ln2_b), ple_w=ple_w.astype(BF16), ple_gate_w=ple_gate_w.astype(BF16),
        ln3_g=row(ln3_g), ln3_b=row(ln3_b))


def _tile(n, pref):
    t = min(n, pref)
    while n % t:
        t -= 8
    return t


def _channel_mix(x2d, o_mla, raw, g, bonus, p_emb, w):
    n = x2d.shape[0]
    x1 = _mix_out(x2d, o_mla, raw, g, bonus, w["gn_g"], w["gn_b"], w["e"], w["et"], w["wo1"], w["wo2"],
                  w["ln1_g"], w["ln1_b"], _tile(n, 256))
    i1, i2, gate = _peer_route(x1, w["peer_wq"], w["peer_k1"], w["peer_k2"], _tile(n, 128))
    wt = _peer_weights(i1, i2, gate, _tile(n, 128))
    y = _peer_dense(x1, wt, w["peer_u"], w["peer_v"], _tile(n, 512), 8)
    return _ple(x1, y, p_emb, w["ln2_g"], w["ln2_b"], w["ple_gate_w"], w["ple_w"], w["ln3_g"], w["ln3_b"],
                _tile(n, 256))


def _prompt_layer(x, p_emb, w):
    b, t, _ = x.shape
    n = b * t
    x2d = x.reshape(n, D_MODEL)
    cq, sq, ck, sk = (jnp.tile(a, (b, 1)) for a in _rope_tables(jnp.arange(t)))
    qn, qr, ckv, kr, kn, v = _mla_proj(x2d, w["w_mla"], w["kv_g"], cq, sq, ck, sk,
                                       (w["w_uk2"], w["w_uv2"]), True, _tile(n, 256))
    r3 = lambda a: a.reshape(b, t, -1)
    o_mla = _mla_prompt_attention(r3(qn), r3(qr), r3(kn), r3(kr), r3(v), _tile(t, 256), _tile(t, 512))
    r, dec, k, vv, nkk, bb, g, bonus, last = _rwkv_proj_prompt(x, w["rw"], _tile(t, 256))
    s0 = jnp.zeros((b, RW_HEADS, RW_HEAD, RW_HEAD), F32)
    raw, wkv = _rwkv_scan(s0, r, dec, k, vv, nkk, bb, b, _tile(t, 128))
    f2 = lambda a: a.reshape(n, -1)
    y = _channel_mix(x2d, f2(o_mla), f2(raw), f2(g), f2(bonus), p_emb.reshape(n, -1), w)
    return y.reshape(b, t, D_MODEL), r3(ckv), r3(kr), wkv, last.reshape(b, RW_COLS)


def _sample_layer(x, p_emb, past_len, cache_ckv, cache_krope, page_table, wkv0, shift0, w):
    db, t, _ = x.shape
    assert t == 1, "the sample path handles one new token per sequence"
    x2d = x.reshape(db, D_MODEL)
    cq, sq, ck, sk = (jnp.tile(a, (db, 1)) for a in _rope_tables(jnp.full((1,), past_len)))
    ql, qr, ckv, kr = _mla_proj(x2d, w["w_mla"], w["kv_g"], cq, sq, ck, sk, (w["w_ukt"],), False, _tile(db, 128))
    o_lat = _mla_sample_attention(page_table, ql.reshape(db, MLA_HEADS, KV_RANK), qr.reshape(db, MLA_HEADS, QK_ROPE),
                                  ckv.reshape(db, 1, KV_RANK), kr.reshape(db, 1, QK_ROPE),
                                  cache_ckv, cache_krope, 8)
    o_mla = _head_up(o_lat.reshape(db, MLA_HEADS * KV_RANK), w["w_uv2"])
    r, dec, k, vv, nkk, bb, g, bonus, last = _rwkv_proj_sample(x2d, shift0, w["rw"])
    s3 = lambda a: a.reshape(db, 1, RW_WIDTH)
    raw, wkv = _rwkv_scan(wkv0, s3(r), s3(dec), s3(k), s3(vv), s3(nkk), s3(bb), _tile(db, 8), 1)
    y = _channel_mix(x2d, o_mla, raw.reshape(db, RW_WIDTH), g, bonus, p_emb.reshape(db, -1), w)
    return (y.reshape(db, 1, D_MODEL), ckv.reshape(db, 1, KV_RANK), kr.reshape(db, 1, QK_ROPE), wkv, last)


def kernel(x_prompt, x_sample, p_prompt, p_sample, cache_ckv, cache_krope, state_wkv, state_shift, page_table, w_in, kv_norm_g, w_uk, w_uv, rw_mu, rw_w0, rw_w_up, rw_a0, rw_a_up, rw_g_up, rw_k_k, rw_k_a, rw_r_k, rw_gn_g, rw_gn_b, w_o, ln1_g, ln1_b, peer_wq, peer_k1, peer_k2, peer_u, peer_v, ln2_g, ln2_b, ple_w, ple_gate_w, ln3_g, ln3_b):
    layer_params = (w_in, kv_norm_g, w_uk, w_uv, rw_mu, rw_w0, rw_w_up, rw_a0, rw_a_up, rw_g_up,
                    rw_k_k, rw_k_a, rw_r_k, rw_gn_g, rw_gn_b, w_o, ln1_g, ln1_b,
                    peer_wq, peer_k1, peer_k2, peer_u, peer_v, ln2_g, ln2_b,
                    ple_w, ple_gate_w, ln3_g, ln3_b)
    depth = w_in.shape[0]
    past_len = page_table.shape[1] * cache_ckv.shape[2]
    xp, xs = x_prompt, x_sample
    outs_p, outs_s = [], []
    for i in range(depth):
        w = _prepare_weights(*(a[i] for a in layer_params))
        xp, *rest_p = _prompt_layer(xp, p_prompt[i], w)
        xs, *rest_s = _sample_layer(xs, p_sample[i], past_len, cache_ckv[i], cache_krope[i], page_table,
                                    state_wkv[i], state_shift[i], w)
        outs_p.append(rest_p)
        outs_s.append(rest_s)
    stack = lambda outs, j: jnp.stack([o[j] for o in outs])
    return (xp, xs, stack(outs_p, 0), stack(outs_p, 1), stack(outs_p, 2), stack(outs_p, 3),
            stack(outs_s, 0), stack(outs_s, 1), stack(outs_s, 2), stack(outs_s, 3))
```

```python
import functools
import math

import jax
import jax.numpy as jnp
from jax import lax
from jax.experimental import pallas as pl
from jax.experimental.pallas import tpu as pltpu

F32 = jnp.float32
BF16 = jnp.bfloat16

D_MODEL = 2048
MLA_HEADS = 8
QK_NOPE = 128
QK_ROPE = 64
V_HEAD = 128
KV_RANK = 512
ROPE_THETA = 10000.0
RW_HEADS = 16
RW_HEAD = 64
RW_WIDTH = RW_HEADS * RW_HEAD
DECAY_LORA = 64
AAA_LORA = 64
GATE_LORA = 160
RW_COLS = 3 * RW_WIDTH + DECAY_LORA + AAA_LORA + GATE_LORA
Q_COLS = MLA_HEADS * (QK_NOPE + QK_ROPE)
MLA_COLS = Q_COLS + KV_RANK + QK_ROPE
PEER_HEADS = 8
N_KEYS = 128
PK_DIM = 256
PK_HALF = PK_DIM // 2
PK_TOPK = 16
KEY_BITS = 7
TOPK_BITS = 4
PLE_DIM = 256
DEPTH = 1
ALPHA = (2 * DEPTH) ** 0.25
LN_EPS = 1e-5
RMS_EPS = 1e-6
GN_EPS = 64e-5
ATTN_SCALE = (QK_NOPE + QK_ROPE) ** -0.5

LANES = 128
RW_PAIRS = RW_HEADS // 2
VMEM_LIMIT = 56 * 1024 * 1024

NT_DIMS = (((1,), (1,)), ((), ()))


def _cparams(sem, vmem=VMEM_LIMIT):
    return pltpu.CompilerParams(dimension_semantics=sem, vmem_limit_bytes=vmem)


def _resident(shape):
    nd = len(shape)
    return pl.BlockSpec(shape, lambda *_: (0,) * nd, pipeline_mode=pl.Buffered(1))


def _nt(a, b):
    return lax.dot_general(a, b, NT_DIMS, preferred_element_type=F32)


def _mm(a, b):
    return jnp.dot(a, b, preferred_element_type=F32)


def _dot01(x, e):
    h = x.astype(BF16)
    r = x - h.astype(F32)
    m = r.astype(BF16)
    l = (r - m.astype(F32)).astype(BF16)
    return _mm(h, e) + _mm(m, e) + _mm(l, e)


def _head_sum(x, e, et):
    return _dot01(_dot01(x, e), et)


def _sigmoid(x):
    return 1.0 / (1.0 + jnp.exp(-x))


def _layer_norm(x, g, b):
    mu = jnp.mean(x, axis=-1, keepdims=True)
    d = x - mu
    var = jnp.mean(d * d, axis=-1, keepdims=True)
    return d * lax.rsqrt(var + LN_EPS) * g + b


def _mla_proj_common(x_ref, w_ref, g_ref, cq_ref, sq_ref, ck_ref, sk_ref):
    x = x_ref[...].astype(BF16)
    res = _mm(x, w_ref[...])
    nq = MLA_HEADS * QK_NOPE
    rq = MLA_HEADS * QK_ROPE
    qn = res[:, :nq]
    qr = res[:, nq:nq + rq] * cq_ref[...] + res[:, nq + rq:nq + 2 * rq] * sq_ref[...]
    o = nq + 2 * rq
    c = res[:, o:o + KV_RANK]
    ckv = c * lax.rsqrt(jnp.mean(c * c, axis=-1, keepdims=True) + RMS_EPS) * g_ref[...]
    o += KV_RANK
    kr = res[:, o:o + QK_ROPE] * ck_ref[...] + res[:, o + QK_ROPE:o + 2 * QK_ROPE] * sk_ref[...]
    return qn, qr, ckv, kr


def _mla_proj_prompt_kernel(x_ref, w_ref, g_ref, cq_ref, sq_ref, ck_ref, sk_ref, wuk_ref, wuv_ref,
                            qn_o, qr_o, ckv_o, kr_o, kn_o, v_o):
    qn, qr, ckv, kr = _mla_proj_common(x_ref, w_ref, g_ref, cq_ref, sq_ref, ck_ref, sk_ref)
    qn_o[...] = qn.astype(BF16)
    qr_o[...] = qr.astype(BF16)
    ckv_o[...] = ckv
    kr_o[...] = kr
    cb = ckv.astype(BF16)
    kn_o[...] = _mm(cb, wuk_ref[...]).astype(BF16)
    v_o[...] = _mm(cb, wuv_ref[...]).astype(BF16)


def _mla_proj_sample_kernel(x_ref, w_ref, g_ref, cq_ref, sq_ref, ck_ref, sk_ref, wukt_ref,
                            ql_o, qr_o, ckv_o, kr_o):
    qn, qr, ckv, kr = _mla_proj_common(x_ref, w_ref, g_ref, cq_ref, sq_ref, ck_ref, sk_ref)
    qr_o[...] = qr.astype(BF16)
    ckv_o[...] = ckv
    kr_o[...] = kr
    for h in range(MLA_HEADS):
        qh = qn[:, h * QK_NOPE:(h + 1) * QK_NOPE].astype(BF16)
        ql_o[:, h * KV_RANK:(h + 1) * KV_RANK] = _mm(qh, wukt_ref[h]).astype(BF16)


def _mla_proj(x2d, w_all, kv_g, cq, sq, ck, sk, up_weights, prompt, tm):
    n = x2d.shape[0]
    wcols = w_all.shape[1]
    nq, rq = MLA_HEADS * QK_NOPE, MLA_HEADS * QK_ROPE
    row = lambda c: pl.BlockSpec((tm, c), lambda i: (i, 0))
    in_specs = [row(D_MODEL), _resident((D_MODEL, wcols)), _resident((1, KV_RANK)),
                row(rq), row(rq), row(QK_ROPE), row(QK_ROPE)]
    if prompt:
        kern = _mla_proj_prompt_kernel
        in_specs += [_resident((KV_RANK, nq)), _resident((KV_RANK, nq))]
        out_shape = [jax.ShapeDtypeStruct((n, nq), BF16), jax.ShapeDtypeStruct((n, rq), BF16),
                     jax.ShapeDtypeStruct((n, KV_RANK), F32), jax.ShapeDtypeStruct((n, QK_ROPE), F32),
                     jax.ShapeDtypeStruct((n, nq), BF16), jax.ShapeDtypeStruct((n, nq), BF16)]
        out_specs = [row(nq), row(rq), row(KV_RANK), row(QK_ROPE), row(nq), row(nq)]
    else:
        kern = _mla_proj_sample_kernel
        in_specs += [_resident((MLA_HEADS, QK_NOPE, KV_RANK))]
        out_shape = [jax.ShapeDtypeStruct((n, MLA_HEADS * KV_RANK), BF16), jax.ShapeDtypeStruct((n, rq), BF16),
                     jax.ShapeDtypeStruct((n, KV_RANK), F32), jax.ShapeDtypeStruct((n, QK_ROPE), F32)]
        out_specs = [row(MLA_HEADS * KV_RANK), row(rq), row(KV_RANK), row(QK_ROPE)]
    return pl.pallas_call(
        kern, grid=(n // tm,), in_specs=in_specs, out_specs=out_specs, out_shape=out_shape,
        compiler_params=_cparams(("parallel",)), name="mla_proj",
    )(x2d, w_all, kv_g, cq, sq, ck, sk, *up_weights)


def _flash_kernel(qn_ref, qr_ref, kn_ref, kr_ref, v_ref, o_ref, m_ref, l_ref, acc_ref, *, tq, tk, nk):
    i = pl.program_id(1)
    j = pl.program_id(2)

    @pl.when(j == 0)
    def _():
        m_ref[...] = jnp.full(m_ref.shape, -jnp.inf, F32)
        l_ref[...] = jnp.zeros(l_ref.shape, F32)
        acc_ref[...] = jnp.zeros(acc_ref.shape, F32)

    @pl.when(j * tk <= i * tq + tq - 1)
    def _():
        krb = kr_ref[0].astype(BF16)
        qpos = i * tq + lax.broadcasted_iota(jnp.int32, (tq, tk), 0)
        kpos = j * tk + lax.broadcasted_iota(jnp.int32, (tq, tk), 1)
        visible = kpos <= qpos
        for h in range(MLA_HEADS):
            ns = slice(h * QK_NOPE, (h + 1) * QK_NOPE)
            s = _nt(qn_ref[0, :, ns], kn_ref[0, :, ns])
            s = s + _nt(qr_ref[0, :, h * QK_ROPE:(h + 1) * QK_ROPE], krb)
            s = jnp.where(visible, s * ATTN_SCALE, -jnp.inf)
            m_prev = m_ref[h]
            m_new = jnp.maximum(m_prev, jnp.max(s, axis=1, keepdims=True))
            alpha = jnp.exp(m_prev - m_new)
            p = jnp.exp(s - m_new)
            l_ref[h] = alpha * l_ref[h] + jnp.sum(p, axis=1, keepdims=True)
            acc_ref[h] = alpha * acc_ref[h] + _mm(p.astype(BF16), v_ref[0, :, h * V_HEAD:(h + 1) * V_HEAD])
            m_ref[h] = m_new

    @pl.when(j == nk - 1)
    def _():
        for h in range(MLA_HEADS):
            o_ref[0, :, h * V_HEAD:(h + 1) * V_HEAD] = (acc_ref[h] / l_ref[h]).astype(o_ref.dtype)


def _mla_prompt_attention(qn, qr, kn, kr, v, tq, tk):
    b, s, _ = qn.shape
    nq, nk = s // tq, s // tk
    last = lambda i: (i * tq + tq - 1) // tk
    qspec = lambda c: pl.BlockSpec((1, tq, c), lambda bb, i, j: (bb, i, 0))
    kspec = lambda c: pl.BlockSpec((1, tk, c), lambda bb, i, j: (bb, jnp.minimum(j, last(i)), 0))
    return pl.pallas_call(
        functools.partial(_flash_kernel, tq=tq, tk=tk, nk=nk),
        grid=(b, nq, nk),
        in_specs=[qspec(qn.shape[2]), qspec(qr.shape[2]), kspec(kn.shape[2]), kspec(kr.shape[2]), kspec(v.shape[2])],
        out_specs=qspec(v.shape[2]),
        out_shape=jax.ShapeDtypeStruct(v.shape, BF16),
        scratch_shapes=[pltpu.VMEM((MLA_HEADS, tq, 1), F32), pltpu.VMEM((MLA_HEADS, tq, 1), F32),
                        pltpu.VMEM((MLA_HEADS, tq, V_HEAD), F32)],
        compiler_params=_cparams(("parallel", "parallel", "arbitrary")), name="mla_prompt_attention",
    )(qn, qr, kn, kr, v)


def _decode_kernel(pt_ref, ql_ref, qr_ref, cn_ref, kn_ref, *rest, pages, ng):
    del pt_ref
    c_refs, k_refs = rest[:pages], rest[pages:2 * pages]
    o_ref, m_ref, l_ref, acc_ref = rest[2 * pages:]
    g = pl.program_id(1)

    @pl.when(g == 0)
    def _():
        m_ref[...] = jnp.full(m_ref.shape, -jnp.inf, F32)
        l_ref[...] = jnp.zeros(l_ref.shape, F32)
        acc_ref[...] = jnp.zeros(acc_ref.shape, F32)

    ql = ql_ref[0]
    qr = qr_ref[0]
    cs, ss = [], []
    for i in range(pages):
        c = c_refs[i][...].astype(BF16)
        k = k_refs[i][...].astype(BF16)
        cs.append(c)
        ss.append(_nt(ql, c) + _mm(qr, k))
    s = jnp.concatenate(ss, axis=1) * ATTN_SCALE
    m_prev = m_ref[...]
    m_new = jnp.maximum(m_prev, jnp.max(s, axis=1, keepdims=True))
    alpha = jnp.exp(m_prev - m_new)
    p = jnp.exp(s - m_new)
    l_new = alpha * l_ref[...] + jnp.sum(p, axis=1, keepdims=True)
    pb = p.astype(BF16)
    page = cs[0].shape[0]
    pv = _mm(pb[:, :page], cs[0])
    for i in range(1, pages):
        pv = pv + _mm(pb[:, i * page:(i + 1) * page], cs[i])
    acc_new = alpha * acc_ref[...] + pv
    m_ref[...] = m_new
    l_ref[...] = l_new
    acc_ref[...] = acc_new

    @pl.when(g == ng - 1)
    def _():
        cn = cn_ref[0].astype(BF16).astype(F32)
        kn = kn_ref[0].astype(BF16).astype(F32)
        s_self = (jnp.sum(ql.astype(F32) * cn, axis=1, keepdims=True)
                  + jnp.sum(qr.astype(F32) * kn, axis=1, keepdims=True)) * ATTN_SCALE
        m_f = jnp.maximum(m_new, s_self)
        a_f = jnp.exp(m_new - m_f)
        p_self = jnp.exp(s_self - m_f)
        l_f = a_f * l_new + p_self
        acc_f = a_f * acc_new + p_self.astype(BF16).astype(F32) * cn
        o_ref[0] = acc_f / l_f


def _mla_sample_attention(page_table, q_lat, q_rope, c_new, kr_new, cache_ckv, cache_krope_t, pages):
    db, n_pages = page_table.shape
    _, page, _ = cache_ckv.shape
    ng = n_pages // pages
    pt = page_table.reshape(-1)
    per_seq = lambda shape: pl.BlockSpec((1,) + shape, lambda b, g, pt_: (b, 0, 0))

    def page_spec(shape, i):
        return pl.BlockSpec((None,) + shape, lambda b, g, pt_: (pt_[b * n_pages + g * pages + i], 0, 0))

    grid_spec = pltpu.PrefetchScalarGridSpec(
        num_scalar_prefetch=1, grid=(db, ng),
        in_specs=[per_seq((MLA_HEADS, KV_RANK)), per_seq((MLA_HEADS, QK_ROPE)),
                  per_seq((1, KV_RANK)), per_seq((1, QK_ROPE))]
                 + [page_spec((page, KV_RANK), i) for i in range(pages)]
                 + [page_spec((QK_ROPE, page), i) for i in range(pages)],
        out_specs=per_seq((MLA_HEADS, KV_RANK)),
        scratch_shapes=[pltpu.VMEM((MLA_HEADS, 1), F32), pltpu.VMEM((MLA_HEADS, 1), F32),
                        pltpu.VMEM((MLA_HEADS, KV_RANK), F32)])
    return pl.pallas_call(
        functools.partial(_decode_kernel, pages=pages, ng=ng),
        grid_spec=grid_spec,
        out_shape=jax.ShapeDtypeStruct((db, MLA_HEADS, KV_RANK), F32),
        compiler_params=_cparams(("parallel", "arbitrary")), name="mla_sample_attention",
    )(pt, q_lat, q_rope, c_new, kr_new, *([cache_ckv] * pages), *([cache_krope_t] * pages))


def _head_up_kernel(o_ref, w_ref, out_ref):
    out_ref[...] = _mm(o_ref[...].astype(BF16), w_ref[...]).astype(out_ref.dtype)


def _head_up(o_lat2d, w_uv2):
    n = o_lat2d.shape[0]
    return pl.pallas_call(
        _head_up_kernel, grid=(MLA_HEADS,),
        in_specs=[pl.BlockSpec((n, KV_RANK), lambda h: (0, h)), pl.BlockSpec((KV_RANK, V_HEAD), lambda h: (0, h))],
        out_specs=pl.BlockSpec((n, V_HEAD), lambda h: (0, h)),
        out_shape=jax.ShapeDtypeStruct((n, MLA_HEADS * V_HEAD), BF16),
        compiler_params=_cparams(("parallel",)), name="mla_head_up",
    )(o_lat2d, w_uv2)


def _rwkv_terms(p, prev, mu_ref, w0_ref, wup_ref, a0_ref, aup_ref, gup_ref, kk_ref, ka_ref, rk_ref,
                e_ref, et_ref, outs):
    r_o, w_o, k_o, v_o, n_o, b_o, g_o, bonus_o = outs
    xs = p + mu_ref[...] * (prev - p)
    w3 = 3 * RW_WIDTH
    r = xs[:, :RW_WIDTH]
    k0 = xs[:, RW_WIDTH:2 * RW_WIDTH]
    v = xs[:, 2 * RW_WIDTH:w3]
    xw = xs[:, w3:w3 + DECAY_LORA]
    xa = xs[:, w3 + DECAY_LORA:w3 + DECAY_LORA + AAA_LORA]
    xg = xs[:, w3 + DECAY_LORA + AAA_LORA:]
    y = w0_ref[...] + _mm(jnp.tanh(xw).astype(BF16), wup_ref[...])
    decay = jnp.exp(-math.exp(-0.5) * _sigmoid(y))
    a = _sigmoid(a0_ref[...] + _mm(xa.astype(BF16), aup_ref[...]))
    g = _mm(_sigmoid(xg).astype(BF16), gup_ref[...])
    kk = k0 * kk_ref[...]
    e, et = e_ref[...], et_ref[...]
    kk = kk * lax.rsqrt(jnp.maximum(_head_sum(kk * kk, e, et), 1e-24))
    k = k0 * (1.0 + (a - 1.0) * ka_ref[...])
    r_o[...] = r
    w_o[...] = decay
    k_o[...] = k
    v_o[...] = v
    n_o[...] = -kk
    b_o[...] = kk * a
    g_o[...] = g
    bonus_o[...] = _head_sum(r * k * rk_ref[...], e, et) * v


def _rwkv_proj_prompt_kernel(x_ref, w_ref, mu_ref, w0_ref, wup_ref, a0_ref, aup_ref, gup_ref,
                             kk_ref, ka_ref, rk_ref, e_ref, et_ref,
                             r_o, w_o, k_o, v_o, n_o, b_o, g_o, bonus_o, last_o, carry_ref):
    tm = x_ref.shape[1]

    @pl.when(pl.program_id(1) == 0)
    def _():
        carry_ref[...] = jnp.zeros(carry_ref.shape, F32)

    p = _mm(x_ref[0].astype(BF16), w_ref[...])
    rows = lax.broadcasted_iota(jnp.int32, p.shape, 0)
    prev = jnp.where(rows == 0, carry_ref[...], pltpu.roll(p, 1, axis=0))
    last = p[tm - 1:tm, :]
    carry_ref[...] = last
    last_o[0] = last
    outs = tuple(o.at[0] for o in (r_o, w_o, k_o, v_o, n_o, b_o, g_o, bonus_o))
    _rwkv_terms(p, prev, mu_ref, w0_ref, wup_ref, a0_ref, aup_ref, gup_ref, kk_ref, ka_ref, rk_ref,
                e_ref, et_ref, outs)


def _rwkv_proj_sample_kernel(x_ref, prev_ref, w_ref, mu_ref, w0_ref, wup_ref, a0_ref, aup_ref, gup_ref,
                             kk_ref, ka_ref, rk_ref, e_ref, et_ref,
                             r_o, w_o, k_o, v_o, n_o, b_o, g_o, bonus_o, last_o):
    p = _mm(x_ref[...].astype(BF16), w_ref[...])
    last_o[...] = p
    _rwkv_terms(p, prev_ref[...], mu_ref, w0_ref, wup_ref, a0_ref, aup_ref, gup_ref, kk_ref, ka_ref, rk_ref,
                e_ref, et_ref, (r_o, w_o, k_o, v_o, n_o, b_o, g_o, bonus_o))


def _rwkv_weight_specs():
    vec = lambda c: _resident((1, c))
    return [_resident((D_MODEL, RW_COLS)), vec(RW_COLS), vec(RW_WIDTH), _resident((DECAY_LORA, RW_WIDTH)),
            vec(RW_WIDTH), _resident((AAA_LORA, RW_WIDTH)), _resident((GATE_LORA, RW_WIDTH)),
            vec(RW_WIDTH), vec(RW_WIDTH), vec(RW_WIDTH),
            _resident((RW_WIDTH, LANES)), _resident((LANES, RW_WIDTH))]


def _rwkv_proj_prompt(x, rw_weights, tm):
    b, t, _ = x.shape
    tile = lambda c: pl.BlockSpec((1, tm, c), lambda bb, i: (bb, i, 0))
    wide = jax.ShapeDtypeStruct((b, t, RW_WIDTH), F32)
    return pl.pallas_call(
        _rwkv_proj_prompt_kernel, grid=(b, t // tm),
        in_specs=[tile(D_MODEL)] + _rwkv_weight_specs(),
        out_specs=[tile(RW_WIDTH)] * 8 + [pl.BlockSpec((1, 1, RW_COLS), lambda bb, i: (bb, 0, 0))],
        out_shape=[wide] * 8 + [jax.ShapeDtypeStruct((b, 1, RW_COLS), F32)],
        scratch_shapes=[pltpu.VMEM((1, RW_COLS), F32)],
        compiler_params=_cparams(("parallel", "arbitrary")), name="rwkv_proj_prompt",
    )(x, *rw_weights)


def _rwkv_proj_sample(x2d, prev, rw_weights):
    n = x2d.shape[0]
    full = lambda c: pl.BlockSpec((n, c), lambda i: (0, 0))
    wide = jax.ShapeDtypeStruct((n, RW_WIDTH), F32)
    return pl.pallas_call(
        _rwkv_proj_sample_kernel, grid=(1,),
        in_specs=[full(D_MODEL), full(RW_COLS)] + _rwkv_weight_specs(),
        out_specs=[full(RW_WIDTH)] * 8 + [full(RW_COLS)],
        out_shape=[wide] * 8 + [jax.ShapeDtypeStruct((n, RW_COLS), F32)],
        compiler_params=_cparams(("arbitrary",)), name="rwkv_proj_sample",
    )(x2d, prev, *rw_weights)


def _rwkv_scan_kernel(s0_ref, r_ref, w_ref, k_ref, v_ref, n_ref, b_ref, e2_ref, out_ref, st_ref, state_ref,
                      *, nb, tc, nc):
    c = pl.program_id(1)
    chains = [(b, p) for b in range(nb) for p in range(RW_PAIRS)]

    @pl.when(c == 0)
    def _():
        for b, p in chains:
            state_ref[b, p] = jnp.concatenate([s0_ref[b, 2 * p], s0_ref[b, 2 * p + 1]], axis=1)

    lane = lax.broadcasted_iota(jnp.int32, (RW_HEAD, LANES), 1)
    sub = lax.broadcasted_iota(jnp.int32, (RW_HEAD, LANES), 0)
    diag = (lane & (RW_HEAD - 1)) == sub
    e2 = e2_ref[...]

    def pieces(x):
        hi = x.astype(BF16)
        return jnp.concatenate([hi, (x - hi.astype(F32)).astype(BF16)], axis=1)

    def head_sums(xs):
        res = _mm(jnp.concatenate([pieces(x) for x in xs], axis=0), e2)
        return [res[i * RW_HEAD:(i + 1) * RW_HEAD] for i in range(len(xs))]

    ins = (r_ref, w_ref, k_ref, v_ref, n_ref, b_ref)
    group = 8 if tc % 8 == 0 else 1

    def steps(t0):
        rows = [[ref[b, pl.ds(t0, group), p * LANES:(p + 1) * LANES] for ref in ins] for b, p in chains]
        outs = [[] for _ in chains]
        for i in range(group):
            r, w, k, v, n, bb = ([x[j][i:i + 1, :] for x in rows] for j in range(len(ins)))
            v_cols = head_sums([jnp.where(diag, vi, 0.0) for vi in v])
            sa = head_sums([state_ref[b, p] * n[ci] for ci, (b, p) in enumerate(chains)])
            new = []
            for ci, (b, p) in enumerate(chains):
                s = state_ref[b, p] * w[ci] + sa[ci] * bb[ci] + v_cols[ci] * k[ci]
                state_ref[b, p] = s
                new.append(s * r[ci])
            for ci, o_col in enumerate(head_sums(new)):
                outs[ci].append(jnp.sum(jnp.where(diag, o_col, 0.0), axis=0, keepdims=True))
        for ci, (b, p) in enumerate(chains):
            out_ref[b, pl.ds(t0, group), p * LANES:(p + 1) * LANES] = (
                outs[ci][0] if group == 1 else jnp.concatenate(outs[ci], axis=0))

    if tc == group:
        steps(0)
    else:
        def body(tg, carry):
            steps(pl.multiple_of(tg * group, group))
            return carry
        lax.fori_loop(0, tc // group, body, 0)

    @pl.when(c == nc - 1)
    def _():
        for b, p in chains:
            s = state_ref[b, p]
            st_ref[b, 2 * p] = s[:, :RW_HEAD]
            st_ref[b, 2 * p + 1] = s[:, RW_HEAD:]


def _rwkv_scan(s0, r, w, k, v, n, bb, e2, nb, tc):
    b, t, _ = r.shape
    nc = t // tc
    seq = pl.BlockSpec((nb, tc, RW_WIDTH), lambda i, c: (i, c, 0))
    st = pl.BlockSpec((nb, RW_HEADS, RW_HEAD, RW_HEAD), lambda i, c: (i, 0, 0, 0))
    return pl.pallas_call(
        functools.partial(_rwkv_scan_kernel, nb=nb, tc=tc, nc=nc),
        grid=(b // nb, nc),
        in_specs=[st] + [seq] * 6 + [_resident(e2.shape)],
        out_specs=[seq, st],
        out_shape=[jax.ShapeDtypeStruct((b, t, RW_WIDTH), F32),
                   jax.ShapeDtypeStruct((b, RW_HEADS, RW_HEAD, RW_HEAD), F32)],
        scratch_shapes=[pltpu.VMEM((nb, RW_PAIRS, RW_HEAD, LANES), F32)],
        compiler_params=_cparams(("parallel", "arbitrary")), name="rwkv_scan",
    )(s0, r, w, k, v, n, bb, e2)


def _mix_out_kernel(x_ref, om_ref, raw_ref, g_ref, bonus_ref, gng_ref, gnb_ref, e_ref, et_ref,
                    wo1_ref, wo2_ref, lng_ref, lnb_ref, o_ref):
    e, et = e_ref[...], et_ref[...]
    raw = raw_ref[...]
    mu = _head_sum(raw, e, et) * (1.0 / RW_HEAD)
    d = raw - mu
    var = _head_sum(d * d, e, et) * (1.0 / RW_HEAD)
    o_rw = (d * lax.rsqrt(var + GN_EPS) * gng_ref[...] + gnb_ref[...] + bonus_ref[...]) * g_ref[...]
    mixed = _mm(om_ref[...], wo1_ref[...]) + _mm(o_rw.astype(BF16), wo2_ref[...])
    o_ref[...] = _layer_norm(ALPHA * x_ref[...] + mixed, lng_ref[...], lnb_ref[...])


def _mix_out(x2d, o_mla, raw, g, bonus, gng, gnb, e, et, wo1, wo2, lng, lnb, tm):
    n = x2d.shape[0]
    row = lambda c: pl.BlockSpec((tm, c), lambda i: (i, 0))
    vec = lambda c: _resident((1, c))
    return pl.pallas_call(
        _mix_out_kernel, grid=(n // tm,),
        in_specs=[row(D_MODEL), row(RW_WIDTH), row(RW_WIDTH), row(RW_WIDTH), row(RW_WIDTH),
                  vec(RW_WIDTH), vec(RW_WIDTH), _resident((RW_WIDTH, LANES)), _resident((LANES, RW_WIDTH)),
                  _resident(wo1.shape), _resident(wo2.shape), vec(D_MODEL), vec(D_MODEL)],
        out_specs=row(D_MODEL),
        out_shape=jax.ShapeDtypeStruct((n, D_MODEL), F32),
        compiler_params=_cparams(("parallel",)), name="mix_out_ln1",
    )(x2d, o_mla, raw, g, bonus, gng, gnb, e, et, wo1, wo2, lng, lnb)


def _peer_route_kernel(x_ref, wq_ref, k1_ref, k2_ref, i1_o, i2_o, g_o):
    tm = x_ref.shape[0]
    kk2 = PK_TOPK * PK_TOPK
    q = _mm(x_ref[...].astype(BF16), wq_ref[...])
    key_row = lax.broadcasted_iota(jnp.int32, (N_KEYS, tm), 0).astype(F32)
    cand_row = lax.broadcasted_iota(jnp.int32, (kk2, tm), 0).astype(F32)
    rank_row = lax.broadcasted_iota(jnp.int32, (PK_TOPK, tm), 0)
    neg = -jnp.inf

    def top_keys(scores):
        def body(r, carry):
            s, vals, idxs = carry
            m = jnp.max(s, axis=0, keepdims=True)
            idx = jnp.min(jnp.where(s == m, key_row, float(N_KEYS)), axis=0, keepdims=True)
            s = jnp.where(key_row == idx, neg, s)
            hit = rank_row == r
            return s, jnp.where(hit, m, vals), jnp.where(hit, idx, idxs)
        zero = jnp.zeros((PK_TOPK, tm), F32)
        _, vals, idxs = lax.fori_loop(0, PK_TOPK, body, (scores, zero, zero))
        return vals, idxs

    experts, gates = [], []
    for h in range(PEER_HEADS):
        q1 = q[:, h * PK_DIM:h * PK_DIM + PK_HALF].astype(BF16)
        q2 = q[:, h * PK_DIM + PK_HALF:(h + 1) * PK_DIM].astype(BF16)
        v1, i1 = top_keys(_nt(k1_ref[h], q1))
        v2, i2 = top_keys(_nt(k2_ref[h], q2))
        cand = jnp.concatenate([v1[a:a + 1, :] + v2 for a in range(PK_TOPK)], axis=0)
        expert = jnp.concatenate([i1[a:a + 1, :] * float(N_KEYS) + i2 for a in range(PK_TOPK)], axis=0)

        def body(r, carry, expert=expert):
            cnd, top, den, e_a, p_a = carry
            m = jnp.max(cnd, axis=0, keepdims=True)
            idx = jnp.min(jnp.where(cnd == m, cand_row, float(kk2)), axis=0, keepdims=True)
            hit = cand_row == idx
            e = jnp.max(jnp.where(hit, expert, -1.0), axis=0, keepdims=True)
            cnd = jnp.where(hit, neg, cnd)
            top = jnp.where(r == 0, m, top)
            pr = jnp.exp(m - top)
            sel = rank_row == r
            return cnd, top, den + pr, jnp.where(sel, e, e_a), jnp.where(sel, pr, p_a)

        zrow = jnp.zeros((1, tm), F32)
        zk = jnp.zeros((PK_TOPK, tm), F32)
        _, _, den, e_h, p_h = lax.fori_loop(0, PK_TOPK, body, (cand, zrow, zrow, zk, zk))
        experts.append(e_h)
        gates.append(p_h / den)
    e_int = jnp.concatenate(experts, axis=0).T.astype(jnp.int32)
    i1_o[...] = e_int >> KEY_BITS
    i2_o[...] = e_int & (N_KEYS - 1)
    g_o[...] = jnp.concatenate(gates, axis=0).T


def _peer_route(x2d, wq, k1, k2, tm):
    n = x2d.shape[0]
    slots = PEER_HEADS * PK_TOPK
    row = lambda c: pl.BlockSpec((tm, c), lambda i: (i, 0))
    return pl.pallas_call(
        _peer_route_kernel, grid=(n // tm,),
        in_specs=[row(D_MODEL), _resident(wq.shape), _resident(k1.shape), _resident(k2.shape)],
        out_specs=[row(slots)] * 3,
        out_shape=[jax.ShapeDtypeStruct((n, slots), jnp.int32), jax.ShapeDtypeStruct((n, slots), jnp.int32),
                   jax.ShapeDtypeStruct((n, slots), F32)],
        compiler_params=_cparams(("parallel",)), name="peer_route",
    )(x2d, wq, k1, k2)


def _peer_weights_kernel(i1_ref, i2_ref, g_ref, wt_ref):
    tm = i1_ref.shape[0]
    key = lax.broadcasted_iota(jnp.int32, (N_KEYS, PEER_HEADS * PK_TOPK), 0)

    def body(t, carry):
        i1 = i1_ref[pl.ds(t, 1), :]
        i2 = i2_ref[pl.ds(t, 1), :]
        g = g_ref[pl.ds(t, 1), :]
        g1 = jnp.where(key == i1, g, 0.0)
        hi = g1.astype(BF16)
        lo = (g1 - hi.astype(F32)).astype(BF16)
        g2 = jnp.where(key == i2, 1.0, 0.0).astype(BF16)
        wt_ref[t] = _nt(jnp.concatenate([hi, lo], axis=1), jnp.concatenate([g2, g2], axis=1))
        return carry

    lax.fori_loop(0, tm, body, 0, unroll=4)


def _peer_weights(i1, i2, g, tm):
    n, slots = i1.shape
    row = pl.BlockSpec((tm, slots), lambda i: (i, 0))
    return pl.pallas_call(
        _peer_weights_kernel, grid=(n // tm,),
        in_specs=[row, row, row],
        out_specs=pl.BlockSpec((tm, N_KEYS, N_KEYS), lambda i: (i, 0, 0)),
        out_shape=jax.ShapeDtypeStruct((n, N_KEYS, N_KEYS), F32),
        compiler_params=_cparams(("parallel",)), name="peer_weights",
    )(i1, i2, g)


def _peer_dense_kernel(x_ref, wt_ref, u_ref, v_ref, y_ref, xb_ref, z_ref, *, rows_per_step):
    c = pl.program_id(1)
    tt = x_ref.shape[0]

    @pl.when(c == 0)
    def _():
        xb_ref[...] = x_ref[...].astype(BF16)
        y_ref[...] = jnp.zeros(y_ref.shape, F32)

    hid = _nt(xb_ref[...], u_ref[...])
    for a in range(rows_per_step):
        h = hid[:, a * N_KEYS:(a + 1) * N_KEYS]
        w = wt_ref[:, a, :]
        gelu = 0.5 * h * (1.0 + lax.erf(h * math.sqrt(0.5)))
        z_ref[:, a * N_KEYS:(a + 1) * N_KEYS] = (w * gelu).astype(BF16)
    y_ref[...] += _mm(z_ref[...], v_ref[...])


def _peer_dense(x2d, wt, u, v, tt, rows_per_step):
    n = x2d.shape[0]
    ne = rows_per_step * N_KEYS
    return pl.pallas_call(
        functools.partial(_peer_dense_kernel, rows_per_step=rows_per_step),
        grid=(n // tt, N_KEYS // rows_per_step),
        in_specs=[pl.BlockSpec((tt, D_MODEL), lambda i, c: (i, 0)),
                  pl.BlockSpec((tt, rows_per_step, N_KEYS), lambda i, c: (i, c, 0)),
                  pl.BlockSpec((ne, D_MODEL), lambda i, c: (c, 0)),
                  pl.BlockSpec((ne, D_MODEL), lambda i, c: (c, 0))],
        out_specs=pl.BlockSpec((tt, D_MODEL), lambda i, c: (i, 0)),
        out_shape=jax.ShapeDtypeStruct((n, D_MODEL), F32),
        scratch_shapes=[pltpu.VMEM((tt, D_MODEL), BF16), pltpu.VMEM((tt, ne), BF16)],
        compiler_params=_cparams(("parallel", "arbitrary")), name="peer_dense",
    )(x2d, wt, u, v)


def _ple_kernel(x1_ref, y_ref, p_ref, l2g_ref, l2b_ref, wg_ref, wp_ref, l3g_ref, l3b_ref, o_ref):
    x2 = _layer_norm(ALPHA * x1_ref[...] + y_ref[...], l2g_ref[...], l2b_ref[...])
    gate = _sigmoid(_mm(x2.astype(BF16), wg_ref[...]))
    ple = _mm(p_ref[...].astype(BF16), wp_ref[...]) * gate
    o_ref[...] = _layer_norm(ALPHA * x2 + ple, l3g_ref[...], l3b_ref[...])


def _ple(x1, y, p_emb, l2g, l2b, wg, wp, l3g, l3b, tm):
    n = x1.shape[0]
    row = lambda c: pl.BlockSpec((tm, c), lambda i: (i, 0))
    vec = _resident((1, D_MODEL))
    return pl.pallas_call(
        _ple_kernel, grid=(n // tm,),
        in_specs=[row(D_MODEL), row(D_MODEL), row(PLE_DIM), vec, vec, _resident(wg.shape), _resident(wp.shape), vec, vec],
        out_specs=row(D_MODEL),
        out_shape=jax.ShapeDtypeStruct((n, D_MODEL), F32),
        compiler_params=_cparams(("parallel",)), name="ple_ln2_ln3",
    )(x1, y, p_emb, l2g, l2b, wg, wp, l3g, l3b)


def _rope_tables(pos):
    inv = ROPE_THETA ** (-jnp.arange(0, QK_ROPE, 2, dtype=F32) / QK_ROPE)
    ang = pos.astype(F32)[:, None] * inv[None, :]
    cos, sin = jnp.cos(ang), jnp.sin(ang)
    ck = jnp.concatenate([cos, cos], axis=-1)
    sk = jnp.concatenate([-sin, sin], axis=-1)
    return jnp.tile(ck, (1, MLA_HEADS)), jnp.tile(sk, (1, MLA_HEADS)), ck, sk


def _prepare_weights(w_in, kv_norm_g, w_uk, w_uv, rw_mu, rw_w0, rw_w_up, rw_a0, rw_a_up, rw_g_up,
                     rw_k_k, rw_k_a, rw_r_k, rw_gn_g, rw_gn_b, w_o, ln1_g, ln1_b,
                     peer_wq, peer_k1, peer_k2, peer_u, peer_v, ln2_g, ln2_b,
                     ple_w, ple_gate_w, ln3_g, ln3_b):
    half = QK_ROPE // 2
    swap = lambda w: jnp.concatenate([w[..., half:], w[..., :half]], axis=-1)
    wq = w_in[:, :Q_COLS].reshape(D_MODEL, MLA_HEADS, QK_NOPE + QK_ROPE)
    wq_n = wq[:, :, :QK_NOPE].reshape(D_MODEL, -1)
    wq_r = wq[:, :, QK_NOPE:]
    w_c = w_in[:, Q_COLS:Q_COLS + KV_RANK]
    w_kr = w_in[:, Q_COLS + KV_RANK:MLA_COLS]
    w_mla = jnp.concatenate([wq_n, wq_r.reshape(D_MODEL, -1), swap(wq_r).reshape(D_MODEL, -1),
                             w_c, w_kr, swap(w_kr)], axis=1).astype(BF16)
    row = lambda a: a.reshape(1, -1)
    head_of_lane = jnp.arange(RW_WIDTH) // RW_HEAD
    e = (head_of_lane[:, None] == jnp.arange(LANES)[None, :]).astype(BF16)
    half_of = (jnp.arange(2 * LANES) % LANES) // RW_HEAD
    e2 = (half_of[:, None] == half_of[None, :LANES]).astype(BF16)
    return dict(
        e2=e2,
        w_mla=w_mla, kv_g=row(kv_norm_g),
        w_uk2=w_uk.reshape(KV_RANK, -1).astype(BF16), w_uv2=w_uv.reshape(KV_RANK, -1).astype(BF16),
        w_ukt=jnp.transpose(w_uk, (1, 2, 0)).astype(BF16),
        rw=(w_in[:, MLA_COLS:].astype(BF16), row(rw_mu), row(rw_w0), rw_w_up.astype(BF16), row(rw_a0),
            rw_a_up.astype(BF16), rw_g_up.astype(BF16), row(rw_k_k), row(rw_k_a), row(rw_r_k), e, e.T),
        gn_g=row(rw_gn_g), gn_b=row(rw_gn_b), e=e, et=e.T,
        wo1=w_o[:MLA_HEADS * V_HEAD].astype(BF16), wo2=w_o[MLA_HEADS * V_HEAD:].astype(BF16),
        ln1_g=row(ln1_g), ln1_b=row(ln1_b),
        peer_wq=peer_wq.astype(BF16), peer_k1=peer_k1.astype(BF16), peer_k2=peer_k2.astype(BF16),
        peer_u=peer_u.astype(BF16), peer_v=peer_v.astype(BF16),
        ln2_g=row(ln2_g), ln2_b=row(ln2_b), ple_w=ple_w.astype(BF16), ple_gate_w=ple_gate_w.astype(BF16),
        ln3_g=row(ln3_g), ln3_b=row(ln3_b))


def _tile(n, pref):
    t = min(n, pref)
    while n % t:
        t -= 8
    return t


def _channel_mix(x2d, o_mla, raw, g, bonus, p_emb, w):
    n = x2d.shape[0]
    x1 = _mix_out(x2d, o_mla, raw, g, bonus, w["gn_g"], w["gn_b"], w["e"], w["et"], w["wo1"], w["wo2"],
                  w["ln1_g"], w["ln1_b"], _tile(n, 256))
    i1, i2, gate = _peer_route(x1, w["peer_wq"], w["peer_k1"], w["peer_k2"], _tile(n, 256))
    wt = _peer_weights(i1, i2, gate, _tile(n, 128))
    y = _peer_dense(x1, wt, w["peer_u"], w["peer_v"], _tile(n, 512), 8)
    return _ple(x1, y, p_emb, w["ln2_g"], w["ln2_b"], w["ple_gate_w"], w["ple_w"], w["ln3_g"], w["ln3_b"],
                _tile(n, 256))


def _prompt_layer(x, p_emb, w):
    b, t, _ = x.shape
    n = b * t
    x2d = x.reshape(n, D_MODEL)
    cq, sq, ck, sk = (jnp.tile(a, (b, 1)) for a in _rope_tables(jnp.arange(t)))
    qn, qr, ckv, kr, kn, v = _mla_proj(x2d, w["w_mla"], w["kv_g"], cq, sq, ck, sk,
                                       (w["w_uk2"], w["w_uv2"]), True, _tile(n, 256))
    r3 = lambda a: a.reshape(b, t, -1)
    o_mla = _mla_prompt_attention(r3(qn), r3(qr), r3(kn), r3(kr), r3(v), _tile(t, 256), _tile(t, 512))
    r, dec, k, vv, nkk, bb, g, bonus, last = _rwkv_proj_prompt(x, w["rw"], _tile(t, 256))
    s0 = jnp.zeros((b, RW_HEADS, RW_HEAD, RW_HEAD), F32)
    raw, wkv = _rwkv_scan(s0, r, dec, k, vv, nkk, bb, w["e2"], b, _tile(t, 128))
    f2 = lambda a: a.reshape(n, -1)
    y = _channel_mix(x2d, f2(o_mla), f2(raw), f2(g), f2(bonus), p_emb.reshape(n, -1), w)
    return y.reshape(b, t, D_MODEL), r3(ckv), r3(kr), wkv, last.reshape(b, RW_COLS)


def _sample_layer(x, p_emb, past_len, cache_ckv, cache_krope, page_table, wkv0, shift0, w):
    db, t, _ = x.shape
    assert t == 1, "the sample path handles one new token per sequence"
    x2d = x.reshape(db, D_MODEL)
    cq, sq, ck, sk = (jnp.tile(a, (db, 1)) for a in _rope_tables(jnp.full((1,), past_len)))
    ql, qr, ckv, kr = _mla_proj(x2d, w["w_mla"], w["kv_g"], cq, sq, ck, sk, (w["w_ukt"],), False, _tile(db, 128))
    o_lat = _mla_sample_attention(page_table, ql.reshape(db, MLA_HEADS, KV_RANK), qr.reshape(db, MLA_HEADS, QK_ROPE),
                                  ckv.reshape(db, 1, KV_RANK), kr.reshape(db, 1, QK_ROPE),
                                  cache_ckv, jnp.swapaxes(cache_krope, 1, 2), 16)
    o_mla = _head_up(o_lat.reshape(db, MLA_HEADS * KV_RANK), w["w_uv2"])
    r, dec, k, vv, nkk, bb, g, bonus, last = _rwkv_proj_sample(x2d, shift0, w["rw"])
    s3 = lambda a: a.reshape(db, 1, RW_WIDTH)
    raw, wkv = _rwkv_scan(wkv0, s3(r), s3(dec), s3(k), s3(vv), s3(nkk), s3(bb), w["e2"], _tile(db, 8), 1)
    y = _channel_mix(x2d, o_mla, raw.reshape(db, RW_WIDTH), g, bonus, p_emb.reshape(db, -1), w)
    return (y.reshape(db, 1, D_MODEL), ckv.reshape(db, 1, KV_RANK), kr.reshape(db, 1, QK_ROPE), wkv, last)


def kernel(x_prompt, x_sample, p_prompt, p_sample, cache_ckv, cache_krope, state_wkv, state_shift, page_table, w_in, kv_norm_g, w_uk, w_uv, rw_mu, rw_w0, rw_w_up, rw_a0, rw_a_up, rw_g_up, rw_k_k, rw_k_a, rw_r_k, rw_gn_g, rw_gn_b, w_o, ln1_g, ln1_b, peer_wq, peer_k1, peer_k2, peer_u, peer_v, ln2_g, ln2_b, ple_w, ple_gate_w, ln3_g, ln3_b):
    layer_params = (w_in, kv_norm_g, w_uk, w_uv, rw_mu, rw_w0, rw_w_up, rw_a0, rw_a_up, rw_g_up,
                    rw_k_k, rw_k_a, rw_r_k, rw_gn_g, rw_gn_b, w_o, ln1_g, ln1_b,
                    peer_wq, peer_k1, peer_k2, peer_u, peer_v, ln2_g, ln2_b,
                    ple_w, ple_gate_w, ln3_g, ln3_b)
    depth = w_in.shape[0]
    past_len = page_table.shape[1] * cache_ckv.shape[2]
    xp, xs = x_prompt, x_sample
    outs_p, outs_s = [], []
    for i in range(depth):
        w = _prepare_weights(*(a[i] for a in layer_params))
        xp, *rest_p = _prompt_layer(xp, p_prompt[i], w)
        xs, *rest_s = _sample_layer(xs, p_sample[i], past_len, cache_ckv[i], cache_krope[i], page_table,
                                    state_wkv[i], state_shift[i], w)
        outs_p.append(rest_p)
        outs_s.append(rest_s)
    stack = lambda outs, j: jnp.stack([o[j] for o in outs])
    return (xp, xs, stack(outs_p, 0), stack(outs_p, 1), stack(outs_p, 2), stack(outs_p, 3),
            stack(outs_s, 0), stack(outs_s, 1), stack(outs_s, 2), stack(outs_s, 3))
```

```python
import functools
import math

import jax
import jax.numpy as jnp
import numpy as np
from jax import lax
from jax.experimental import pallas as pl
from jax.experimental.pallas import tpu as pltpu

F32 = jnp.float32
BF16 = jnp.bfloat16

D_MODEL = 2048
MLA_HEADS = 8
QK_NOPE = 128
QK_ROPE = 64
V_HEAD = 128
KV_RANK = 512
ROPE_THETA = 10000.0
RW_HEADS = 16
RW_HEAD = 64
RW_WIDTH = RW_HEADS * RW_HEAD
DECAY_LORA = 64
AAA_LORA = 64
GATE_LORA = 160
RW_COLS = 3 * RW_WIDTH + DECAY_LORA + AAA_LORA + GATE_LORA
Q_COLS = MLA_HEADS * (QK_NOPE + QK_ROPE)
MLA_COLS = Q_COLS + KV_RANK + QK_ROPE
PEER_HEADS = 8
N_KEYS = 128
PK_DIM = 256
PK_HALF = PK_DIM // 2
PK_TOPK = 16
KEY_BITS = 7
TOPK_BITS = 4
PLE_DIM = 256
DEPTH = 1
ALPHA = (2 * DEPTH) ** 0.25
LN_EPS = 1e-5
RMS_EPS = 1e-6
GN_EPS = 64e-5
ATTN_SCALE = (QK_NOPE + QK_ROPE) ** -0.5

LANES = 128
SUBLANES = 8
RW_PAIRS = RW_HEADS // 2
DECODE_PAGES = 32
SCAN_DOT_PARTS = 4
VMEM_LIMIT = 56 * 1024 * 1024

NT_DIMS = (((1,), (1,)), ((), ()))


def _cparams(sem, vmem=VMEM_LIMIT):
    return pltpu.CompilerParams(dimension_semantics=sem, vmem_limit_bytes=vmem)


def _resident(shape):
    nd = len(shape)
    return pl.BlockSpec(shape, lambda *_: (0,) * nd, pipeline_mode=pl.Buffered(1))


def _nt(a, b):
    return lax.dot_general(a, b, NT_DIMS, preferred_element_type=F32)


def _mm(a, b):
    return jnp.dot(a, b, preferred_element_type=F32)


def _dot01(x, e):
    h = x.astype(BF16)
    r = x - h.astype(F32)
    m = r.astype(BF16)
    l = (r - m.astype(F32)).astype(BF16)
    return _mm(h, e) + _mm(m, e) + _mm(l, e)


def _head_sum(x, e, et):
    return _dot01(_dot01(x, e), et)


def _sigmoid(x):
    return 1.0 / (1.0 + jnp.exp(-x))


def _layer_norm(x, g, b):
    mu = jnp.mean(x, axis=-1, keepdims=True)
    d = x - mu
    var = jnp.mean(d * d, axis=-1, keepdims=True)
    return d * lax.rsqrt(var + LN_EPS) * g + b


def _mla_proj_common(x_ref, w_ref, g_ref, cq_ref, sq_ref, ck_ref, sk_ref):
    x = x_ref[...].astype(BF16)
    res = _mm(x, w_ref[...])
    nq = MLA_HEADS * QK_NOPE
    rq = MLA_HEADS * QK_ROPE
    qn = res[:, :nq]
    qr = res[:, nq:nq + rq] * cq_ref[...] + res[:, nq + rq:nq + 2 * rq] * sq_ref[...]
    o = nq + 2 * rq
    c = res[:, o:o + KV_RANK]
    ckv = c * lax.rsqrt(jnp.mean(c * c, axis=-1, keepdims=True) + RMS_EPS) * g_ref[...]
    o += KV_RANK
    kr = res[:, o:o + QK_ROPE] * ck_ref[...] + res[:, o + QK_ROPE:o + 2 * QK_ROPE] * sk_ref[...]
    return qn, qr, ckv, kr


def _mla_proj_prompt_kernel(x_ref, w_ref, g_ref, cq_ref, sq_ref, ck_ref, sk_ref, wuk_ref, wuv_ref,
                            qn_o, qr_o, ckv_o, kr_o, kn_o, v_o):
    qn, qr, ckv, kr = _mla_proj_common(x_ref, w_ref, g_ref, cq_ref, sq_ref, ck_ref, sk_ref)
    qn_o[...] = qn.astype(BF16)
    qr_o[...] = qr.astype(BF16)
    ckv_o[...] = ckv
    kr_o[...] = kr
    cb = ckv.astype(BF16)
    kn_o[...] = _mm(cb, wuk_ref[...]).astype(BF16)
    v_o[...] = _mm(cb, wuv_ref[...]).astype(BF16)


def _mla_proj_sample_kernel(x_ref, w_ref, g_ref, cq_ref, sq_ref, ck_ref, sk_ref, wukt_ref,
                            ql_o, qr_o, ckv_o, kr_o):
    qn, qr, ckv, kr = _mla_proj_common(x_ref, w_ref, g_ref, cq_ref, sq_ref, ck_ref, sk_ref)
    qr_o[...] = qr.astype(BF16)
    ckv_o[...] = ckv
    kr_o[...] = kr
    for h in range(MLA_HEADS):
        qh = qn[:, h * QK_NOPE:(h + 1) * QK_NOPE].astype(BF16)
        ql_o[:, h * KV_RANK:(h + 1) * KV_RANK] = _mm(qh, wukt_ref[h]).astype(BF16)


def _mla_proj(x2d, w_all, kv_g, cq, sq, ck, sk, up_weights, prompt, tm):
    n = x2d.shape[0]
    wcols = w_all.shape[1]
    nq, rq = MLA_HEADS * QK_NOPE, MLA_HEADS * QK_ROPE
    row = lambda c: pl.BlockSpec((tm, c), lambda i: (i, 0))
    in_specs = [row(D_MODEL), _resident((D_MODEL, wcols)), _resident((1, KV_RANK)),
                row(rq), row(rq), row(QK_ROPE), row(QK_ROPE)]
    if prompt:
        kern = _mla_proj_prompt_kernel
        in_specs += [_resident((KV_RANK, nq)), _resident((KV_RANK, nq))]
        out_shape = [jax.ShapeDtypeStruct((n, nq), BF16), jax.ShapeDtypeStruct((n, rq), BF16),
                     jax.ShapeDtypeStruct((n, KV_RANK), F32), jax.ShapeDtypeStruct((n, QK_ROPE), F32),
                     jax.ShapeDtypeStruct((n, nq), BF16), jax.ShapeDtypeStruct((n, nq), BF16)]
        out_specs = [row(nq), row(rq), row(KV_RANK), row(QK_ROPE), row(nq), row(nq)]
    else:
        kern = _mla_proj_sample_kernel
        in_specs += [_resident((MLA_HEADS, QK_NOPE, KV_RANK))]
        out_shape = [jax.ShapeDtypeStruct((n, MLA_HEADS * KV_RANK), BF16), jax.ShapeDtypeStruct((n, rq), BF16),
                     jax.ShapeDtypeStruct((n, KV_RANK), F32), jax.ShapeDtypeStruct((n, QK_ROPE), F32)]
        out_specs = [row(MLA_HEADS * KV_RANK), row(rq), row(KV_RANK), row(QK_ROPE)]
    return pl.pallas_call(
        kern, grid=(n // tm,), in_specs=in_specs, out_specs=out_specs, out_shape=out_shape,
        compiler_params=_cparams(("parallel",)), name="mla_proj",
    )(x2d, w_all, kv_g, cq, sq, ck, sk, *up_weights)


def _flash_kernel(qn_ref, qr_ref, kn_ref, kr_ref, v_ref, o_ref, m_ref, l_ref, acc_ref, *, tq, tk, nk):
    i = pl.program_id(1)
    j = pl.program_id(2)

    @pl.when(j == 0)
    def _():
        m_ref[...] = jnp.full(m_ref.shape, -jnp.inf, F32)
        l_ref[...] = jnp.zeros(l_ref.shape, F32)
        acc_ref[...] = jnp.zeros(acc_ref.shape, F32)

    def sweep(masked):
        krb = kr_ref[0].astype(BF16)
        if masked:
            qpos = i * tq + lax.broadcasted_iota(jnp.int32, (tq, tk), 0)
            kpos = j * tk + lax.broadcasted_iota(jnp.int32, (tq, tk), 1)
            visible = kpos <= qpos
        for h in range(MLA_HEADS):
            ns = slice(h * QK_NOPE, (h + 1) * QK_NOPE)
            s = _nt(qn_ref[0, :, ns], kn_ref[0, :, ns])
            s = (s + _nt(qr_ref[0, :, h * QK_ROPE:(h + 1) * QK_ROPE], krb)) * ATTN_SCALE
            if masked:
                s = jnp.where(visible, s, -jnp.inf)
            m_prev = m_ref[h]
            m_new = jnp.maximum(m_prev, jnp.max(s, axis=1, keepdims=True))
            alpha = jnp.exp(m_prev - m_new)
            p = jnp.exp(s - m_new)
            l_ref[h] = alpha * l_ref[h] + jnp.sum(p, axis=1, keepdims=True)
            acc_ref[h] = alpha * acc_ref[h] + _mm(p.astype(BF16), v_ref[0, :, h * V_HEAD:(h + 1) * V_HEAD])
            m_ref[h] = m_new

    first_q, last_q = i * tq, i * tq + tq - 1
    first_k, last_k = j * tk, j * tk + tk - 1

    @pl.when(last_k <= first_q)
    def _():
        sweep(False)

    @pl.when((last_k > first_q) & (first_k <= last_q))
    def _():
        sweep(True)

    @pl.when(j == nk - 1)
    def _():
        for h in range(MLA_HEADS):
            o_ref[0, :, h * V_HEAD:(h + 1) * V_HEAD] = (acc_ref[h] / l_ref[h]).astype(o_ref.dtype)


def _mla_prompt_attention(qn, qr, kn, kr, v, tq, tk):
    b, s, _ = qn.shape
    nq, nk = s // tq, s // tk
    last = lambda i: (i * tq + tq - 1) // tk
    qspec = lambda c: pl.BlockSpec((1, tq, c), lambda bb, i, j: (bb, i, 0))
    kspec = lambda c: pl.BlockSpec((1, tk, c), lambda bb, i, j: (bb, jnp.minimum(j, last(i)), 0))
    return pl.pallas_call(
        functools.partial(_flash_kernel, tq=tq, tk=tk, nk=nk),
        grid=(b, nq, nk),
        in_specs=[qspec(qn.shape[2]), qspec(qr.shape[2]), kspec(kn.shape[2]), kspec(kr.shape[2]), kspec(v.shape[2])],
        out_specs=qspec(v.shape[2]),
        out_shape=jax.ShapeDtypeStruct(v.shape, BF16),
        scratch_shapes=[pltpu.VMEM((MLA_HEADS, tq, 1), F32), pltpu.VMEM((MLA_HEADS, tq, 1), F32),
                        pltpu.VMEM((MLA_HEADS, tq, V_HEAD), F32)],
        compiler_params=_cparams(("parallel", "parallel", "arbitrary")), name="mla_prompt_attention",
    )(qn, qr, kn, kr, v)


def _decode_kernel(pt_ref, ql_ref, qr_ref, cn_ref, kn_ref, *rest, pages, ng):
    del pt_ref
    c_refs, k_refs = rest[:pages], rest[pages:2 * pages]
    o_ref, m_ref, l_ref, acc_ref = rest[2 * pages:]
    g = pl.program_id(1)

    @pl.when(g == 0)
    def _():
        m_ref[...] = jnp.full(m_ref.shape, -jnp.inf, F32)
        l_ref[...] = jnp.zeros(l_ref.shape, F32)
        acc_ref[...] = jnp.zeros(acc_ref.shape, F32)

    ql = ql_ref[0]
    qr = qr_ref[0]
    cs, ss = [], []
    for i in range(pages):
        c = c_refs[i][...].astype(BF16)
        k = k_refs[i][...].astype(BF16)
        cs.append(c)
        ss.append(_nt(ql, c) + _mm(qr, k))
    s = jnp.concatenate(ss, axis=1) * ATTN_SCALE
    m_prev = m_ref[...]
    m_new = jnp.maximum(m_prev, jnp.max(s, axis=1, keepdims=True))
    alpha = jnp.exp(m_prev - m_new)
    p = jnp.exp(s - m_new)
    l_new = alpha * l_ref[...] + jnp.sum(p, axis=1, keepdims=True)
    pb = p.astype(BF16)
    page = cs[0].shape[0]
    pv = _mm(pb[:, :page], cs[0])
    for i in range(1, pages):
        pv = pv + _mm(pb[:, i * page:(i + 1) * page], cs[i])
    acc_new = alpha * acc_ref[...] + pv
    m_ref[...] = m_new
    l_ref[...] = l_new
    acc_ref[...] = acc_new

    @pl.when(g == ng - 1)
    def _():
        cn = cn_ref[0].astype(BF16).astype(F32)
        kn = kn_ref[0].astype(BF16).astype(F32)
        s_self = (jnp.sum(ql.astype(F32) * cn, axis=1, keepdims=True)
                  + jnp.sum(qr.astype(F32) * kn, axis=1, keepdims=True)) * ATTN_SCALE
        m_f = jnp.maximum(m_new, s_self)
        a_f = jnp.exp(m_new - m_f)
        p_self = jnp.exp(s_self - m_f)
        l_f = a_f * l_new + p_self
        acc_f = a_f * acc_new + p_self.astype(BF16).astype(F32) * cn
        o_ref[0] = acc_f / l_f


def _mla_sample_attention(page_table, q_lat, q_rope, c_new, kr_new, cache_ckv, cache_krope_t, pages):
    db, n_pages = page_table.shape
    _, page, _ = cache_ckv.shape
    ng = n_pages // pages
    pt = page_table.reshape(-1)
    per_seq = lambda shape: pl.BlockSpec((1,) + shape, lambda b, g, pt_: (b, 0, 0))

    def page_spec(shape, i):
        return pl.BlockSpec((None,) + shape, lambda b, g, pt_: (pt_[b * n_pages + g * pages + i], 0, 0))

    grid_spec = pltpu.PrefetchScalarGridSpec(
        num_scalar_prefetch=1, grid=(db, ng),
        in_specs=[per_seq((MLA_HEADS, KV_RANK)), per_seq((MLA_HEADS, QK_ROPE)),
                  per_seq((1, KV_RANK)), per_seq((1, QK_ROPE))]
                 + [page_spec((page, KV_RANK), i) for i in range(pages)]
                 + [page_spec((QK_ROPE, page), i) for i in range(pages)],
        out_specs=per_seq((MLA_HEADS, KV_RANK)),
        scratch_shapes=[pltpu.VMEM((MLA_HEADS, 1), F32), pltpu.VMEM((MLA_HEADS, 1), F32),
                        pltpu.VMEM((MLA_HEADS, KV_RANK), F32)])
    return pl.pallas_call(
        functools.partial(_decode_kernel, pages=pages, ng=ng),
        grid_spec=grid_spec,
        out_shape=jax.ShapeDtypeStruct((db, MLA_HEADS, KV_RANK), F32),
        compiler_params=_cparams(("parallel", "arbitrary")), name="mla_sample_attention",
    )(pt, q_lat, q_rope, c_new, kr_new, *([cache_ckv] * pages), *([cache_krope_t] * pages))


def _head_up_kernel(o_ref, w_ref, out_ref):
    out_ref[...] = _mm(o_ref[...].astype(BF16), w_ref[...]).astype(out_ref.dtype)


def _head_up(o_lat2d, w_uv2):
    n = o_lat2d.shape[0]
    return pl.pallas_call(
        _head_up_kernel, grid=(MLA_HEADS,),
        in_specs=[pl.BlockSpec((n, KV_RANK), lambda h: (0, h)), pl.BlockSpec((KV_RANK, V_HEAD), lambda h: (0, h))],
        out_specs=pl.BlockSpec((n, V_HEAD), lambda h: (0, h)),
        out_shape=jax.ShapeDtypeStruct((n, MLA_HEADS * V_HEAD), BF16),
        compiler_params=_cparams(("parallel",)), name="mla_head_up",
    )(o_lat2d, w_uv2)


def _rwkv_terms(p, prev, mu_ref, w0_ref, wup_ref, a0_ref, aup_ref, gup_ref, kk_ref, ka_ref, rk_ref,
                e_ref, et_ref, outs):
    r_o, w_o, k_o, v_o, n_o, b_o, g_o, bonus_o = outs
    xs = p + mu_ref[...] * (prev - p)
    w3 = 3 * RW_WIDTH
    r = xs[:, :RW_WIDTH]
    k0 = xs[:, RW_WIDTH:2 * RW_WIDTH]
    v = xs[:, 2 * RW_WIDTH:w3]
    xw = xs[:, w3:w3 + DECAY_LORA]
    xa = xs[:, w3 + DECAY_LORA:w3 + DECAY_LORA + AAA_LORA]
    xg = xs[:, w3 + DECAY_LORA + AAA_LORA:]
    y = w0_ref[...] + _mm(jnp.tanh(xw).astype(BF16), wup_ref[...])
    decay = jnp.exp(-math.exp(-0.5) * _sigmoid(y))
    a = _sigmoid(a0_ref[...] + _mm(xa.astype(BF16), aup_ref[...]))
    g = _mm(_sigmoid(xg).astype(BF16), gup_ref[...])
    kk = k0 * kk_ref[...]
    e, et = e_ref[...], et_ref[...]
    kk = kk * lax.rsqrt(jnp.maximum(_head_sum(kk * kk, e, et), 1e-24))
    k = k0 * (1.0 + (a - 1.0) * ka_ref[...])
    r_o[...] = r
    w_o[...] = decay
    k_o[...] = k
    v_o[...] = v
    n_o[...] = -kk
    b_o[...] = kk * a
    g_o[...] = g
    bonus_o[...] = _head_sum(r * k * rk_ref[...], e, et) * v


def _rwkv_proj_prompt_kernel(x_ref, w_ref, mu_ref, w0_ref, wup_ref, a0_ref, aup_ref, gup_ref,
                             kk_ref, ka_ref, rk_ref, e_ref, et_ref,
                             r_o, w_o, k_o, v_o, n_o, b_o, g_o, bonus_o, last_o, carry_ref):
    tm = x_ref.shape[1]

    @pl.when(pl.program_id(1) == 0)
    def _():
        carry_ref[...] = jnp.zeros(carry_ref.shape, F32)

    p = _mm(x_ref[0].astype(BF16), w_ref[...])
    rows = lax.broadcasted_iota(jnp.int32, p.shape, 0)
    prev = jnp.where(rows == 0, carry_ref[...], pltpu.roll(p, 1, axis=0))
    last = p[tm - 1:tm, :]
    carry_ref[...] = last
    last_o[0] = last
    outs = tuple(o.at[0] for o in (r_o, w_o, k_o, v_o, n_o, b_o, g_o, bonus_o))
    _rwkv_terms(p, prev, mu_ref, w0_ref, wup_ref, a0_ref, aup_ref, gup_ref, kk_ref, ka_ref, rk_ref,
                e_ref, et_ref, outs)


def _rwkv_proj_sample_kernel(x_ref, prev_ref, w_ref, mu_ref, w0_ref, wup_ref, a0_ref, aup_ref, gup_ref,
                             kk_ref, ka_ref, rk_ref, e_ref, et_ref,
                             r_o, w_o, k_o, v_o, n_o, b_o, g_o, bonus_o, last_o):
    p = _mm(x_ref[...].astype(BF16), w_ref[...])
    last_o[...] = p
    _rwkv_terms(p, prev_ref[...], mu_ref, w0_ref, wup_ref, a0_ref, aup_ref, gup_ref, kk_ref, ka_ref, rk_ref,
                e_ref, et_ref, (r_o, w_o, k_o, v_o, n_o, b_o, g_o, bonus_o))


def _rwkv_weight_specs():
    vec = lambda c: _resident((1, c))
    return [_resident((D_MODEL, RW_COLS)), vec(RW_COLS), vec(RW_WIDTH), _resident((DECAY_LORA, RW_WIDTH)),
            vec(RW_WIDTH), _resident((AAA_LORA, RW_WIDTH)), _resident((GATE_LORA, RW_WIDTH)),
            vec(RW_WIDTH), vec(RW_WIDTH), vec(RW_WIDTH),
            _resident((RW_WIDTH, LANES)), _resident((LANES, RW_WIDTH))]


def _rwkv_proj_prompt(x, rw_weights, tm):
    b, t, _ = x.shape
    tile = lambda c: pl.BlockSpec((1, tm, c), lambda bb, i: (bb, i, 0))
    wide = jax.ShapeDtypeStruct((b, t, RW_WIDTH), F32)
    return pl.pallas_call(
        _rwkv_proj_prompt_kernel, grid=(b, t // tm),
        in_specs=[tile(D_MODEL)] + _rwkv_weight_specs(),
        out_specs=[tile(RW_WIDTH)] * 8 + [pl.BlockSpec((1, 1, RW_COLS), lambda bb, i: (bb, 0, 0))],
        out_shape=[wide] * 8 + [jax.ShapeDtypeStruct((b, 1, RW_COLS), F32)],
        scratch_shapes=[pltpu.VMEM((1, RW_COLS), F32)],
        compiler_params=_cparams(("parallel", "arbitrary")), name="rwkv_proj_prompt",
    )(x, *rw_weights)


def _rwkv_proj_sample(x2d, prev, rw_weights):
    n = x2d.shape[0]
    full = lambda c: pl.BlockSpec((n, c), lambda i: (0, 0))
    wide = jax.ShapeDtypeStruct((n, RW_WIDTH), F32)
    return pl.pallas_call(
        _rwkv_proj_sample_kernel, grid=(1,),
        in_specs=[full(D_MODEL), full(RW_COLS)] + _rwkv_weight_specs(),
        out_specs=[full(RW_WIDTH)] * 8 + [full(RW_COLS)],
        out_shape=[wide] * 8 + [jax.ShapeDtypeStruct((n, RW_COLS), F32)],
        compiler_params=_cparams(("arbitrary",)), name="rwkv_proj_sample",
    )(x2d, prev, *rw_weights)


def _rwkv_scan_kernel(s0_ref, r_ref, w_ref, k_ref, v_ref, n_ref, b_ref, e2_ref, eo_ref, *rest, nb, tc, nc):
    c = pl.program_id(1)
    chains = [(b, p) for b in range(nb) for p in range(RW_PAIRS)]
    group = 8 if tc % 8 == 0 else 1
    if group > 1:
        vt_ref, ev_ref, out_ref, st_ref, state_ref, vcol_ref = rest
    else:
        out_ref, st_ref, state_ref = rest

    @pl.when(c == 0)
    def _():
        for b, p in chains:
            state_ref[b, p] = jnp.concatenate([s0_ref[b, 2 * p], s0_ref[b, 2 * p + 1]], axis=1)

    lane = lax.broadcasted_iota(jnp.int32, (RW_HEAD, LANES), 1)
    sub = lax.broadcasted_iota(jnp.int32, (RW_HEAD, LANES), 0)
    diag = (lane & (RW_HEAD - 1)) == sub
    e2 = e2_ref[...]

    def pieces(x):
        hi = x.astype(BF16)
        return jnp.concatenate([hi, (x - hi.astype(F32)).astype(BF16)], axis=1)

    def head_sums(xs):
        out = []
        per = max(len(xs) // SCAN_DOT_PARTS, 1)
        for q in range(0, len(xs), per):
            res = _mm(jnp.concatenate([pieces(x) for x in xs[q:q + per]], axis=0), e2)
            out += [res[i * RW_HEAD:(i + 1) * RW_HEAD] for i in range(per)]
        return out

    def head_sums_out(xs):
        eo = eo_ref[...]
        out = []
        per = max(len(xs) // SCAN_DOT_PARTS, 2)
        for q in range(0, len(xs), per):
            res = _mm(jnp.concatenate([jnp.concatenate([xs[i].astype(BF16), xs[i + 1].astype(BF16)], axis=1)
                                       for i in range(q, q + per, 2)], axis=0), eo)
            for i in range(per // 2):
                blk = res[i * RW_HEAD:(i + 1) * RW_HEAD]
                out += [blk[:, :LANES], blk[:, LANES:]]
        return out

    ins = (r_ref, w_ref, k_ref, v_ref, n_ref, b_ref)

    def steps(tg):
        t0 = tg * group if isinstance(tg, int) else pl.multiple_of(tg * group, group)
        rows = [[ref[b, pl.ds(t0, group), p * LANES:(p + 1) * LANES] for ref in ins] for b, p in chains]
        outs = [[] for _ in chains]
        if group > 1:
            lhs = [pieces(jnp.where(lane // (2 * group) == p, vt_ref[b, tg], 0.0)) for b, p in chains]
            vcol_ref[...] = _mm(jnp.concatenate(lhs, axis=0), ev_ref[...])
        for i in range(group):
            r, w, k, v, n, bb = ([x[j][i:i + 1, :] for x in rows] for j in range(len(ins)))
            if group > 1:
                v_cols = [vcol_ref[ci * RW_HEAD:(ci + 1) * RW_HEAD, i * LANES:(i + 1) * LANES]
                          for ci in range(len(chains))]
            else:
                v_cols = head_sums([jnp.where(diag, vi, 0.0) for vi in v])
            sa = head_sums([state_ref[b, p] * n[ci] for ci, (b, p) in enumerate(chains)])
            new = []
            for ci, (b, p) in enumerate(chains):
                s = state_ref[b, p] * w[ci] + sa[ci] * bb[ci] + v_cols[ci] * k[ci]
                state_ref[b, p] = s
                new.append(s * r[ci])
            for ci, o_col in enumerate(head_sums_out(new)):
                outs[ci].append(jnp.sum(jnp.where(diag, o_col, 0.0), axis=0, keepdims=True))
        for ci, (b, p) in enumerate(chains):
            out_ref[b, pl.ds(t0, group), p * LANES:(p + 1) * LANES] = (
                outs[ci][0] if group == 1 else jnp.concatenate(outs[ci], axis=0))

    if tc == group:
        steps(0)
    else:
        def body(tg, carry):
            steps(tg)
            return carry
        lax.fori_loop(0, tc // group, body, 0)

    @pl.when(c == nc - 1)
    def _():
        for b, p in chains:
            s = state_ref[b, p]
            st_ref[b, 2 * p] = s[:, :RW_HEAD]
            st_ref[b, 2 * p + 1] = s[:, RW_HEAD:]


def _rwkv_scan(s0, r, w, k, v, n, bb, e2, nb, tc):
    b, t, _ = r.shape
    nc = t // tc
    seq = pl.BlockSpec((nb, tc, RW_WIDTH), lambda i, c: (i, c, 0))
    st = pl.BlockSpec((nb, RW_HEADS, RW_HEAD, RW_HEAD), lambda i, c: (i, 0, 0, 0))
    same_head = e2[:LANES]
    zero = jnp.zeros_like(same_head)
    eo = jnp.concatenate([jnp.concatenate([same_head, zero], axis=1),
                          jnp.concatenate([zero, same_head], axis=1)], axis=0)
    in_specs = [st] + [seq] * 6 + [_resident(e2.shape), _resident(eo.shape)]
    args = [s0, r, w, k, v, n, bb, e2, eo]
    scratch = [pltpu.VMEM((nb, RW_PAIRS, RW_HEAD, LANES), F32)]
    if tc % 8 == 0:
        grp = 8
        vt = v.reshape(b, t // grp, grp, RW_PAIRS, 2, RW_HEAD).transpose(0, 1, 5, 3, 2, 4)
        vt = vt.reshape(b, t // grp, RW_HEAD, LANES)
        src = jnp.arange(2 * LANES) % LANES
        dst = jnp.arange(grp * LANES)
        ev = ((src[:, None] % (2 * grp)) // 2 == dst[None, :] // LANES) & (src[:, None] % 2 == (dst[None, :] % LANES) // RW_HEAD)
        in_specs += [pl.BlockSpec((nb, tc // grp, RW_HEAD, LANES), lambda i, c: (i, c, 0, 0)),
                     _resident((2 * LANES, grp * LANES))]
        args += [vt, ev.astype(BF16)]
        scratch.append(pltpu.VMEM((nb * RW_PAIRS * RW_HEAD, grp * LANES), F32))
    return pl.pallas_call(
        functools.partial(_rwkv_scan_kernel, nb=nb, tc=tc, nc=nc),
        grid=(b // nb, nc),
        in_specs=in_specs,
        out_specs=[seq, st],
        out_shape=[jax.ShapeDtypeStruct((b, t, RW_WIDTH), F32),
                   jax.ShapeDtypeStruct((b, RW_HEADS, RW_HEAD, RW_HEAD), F32)],
        scratch_shapes=scratch,
        compiler_params=_cparams(("parallel", "arbitrary")), name="rwkv_scan",
    )(*args)


def _mix_out_kernel(x_ref, om_ref, raw_ref, g_ref, bonus_ref, gng_ref, gnb_ref, e_ref, et_ref,
                    wo1_ref, wo2_ref, lng_ref, lnb_ref, o_ref):
    e, et = e_ref[...], et_ref[...]
    raw = raw_ref[...]
    mu = _head_sum(raw, e, et) * (1.0 / RW_HEAD)
    d = raw - mu
    var = _head_sum(d * d, e, et) * (1.0 / RW_HEAD)
    o_rw = (d * lax.rsqrt(var + GN_EPS) * gng_ref[...] + gnb_ref[...] + bonus_ref[...]) * g_ref[...]
    mixed = _mm(om_ref[...], wo1_ref[...]) + _mm(o_rw.astype(BF16), wo2_ref[...])
    o_ref[...] = _layer_norm(ALPHA * x_ref[...] + mixed, lng_ref[...], lnb_ref[...])


def _mix_out(x2d, o_mla, raw, g, bonus, gng, gnb, e, et, wo1, wo2, lng, lnb, tm):
    n = x2d.shape[0]
    row = lambda c: pl.BlockSpec((tm, c), lambda i: (i, 0))
    vec = lambda c: _resident((1, c))
    return pl.pallas_call(
        _mix_out_kernel, grid=(n // tm,),
        in_specs=[row(D_MODEL), row(RW_WIDTH), row(RW_WIDTH), row(RW_WIDTH), row(RW_WIDTH),
                  vec(RW_WIDTH), vec(RW_WIDTH), _resident((RW_WIDTH, LANES)), _resident((LANES, RW_WIDTH)),
                  _resident(wo1.shape), _resident(wo2.shape), vec(D_MODEL), vec(D_MODEL)],
        out_specs=row(D_MODEL),
        out_shape=jax.ShapeDtypeStruct((n, D_MODEL), F32),
        compiler_params=_cparams(("parallel",)), name="mix_out_ln1",
    )(x2d, o_mla, raw, g, bonus, gng, gnb, e, et, wo1, wo2, lng, lnb)


def _peer_route_kernel(x_ref, wq_ref, k1_ref, k2_ref, pa_ref, pb_ref, flat_ref, i1_o, i2_o, g_o):
    tm = x_ref.shape[0]
    kk2 = PK_TOPK * PK_TOPK
    q = _mm(x_ref[...].astype(BF16), wq_ref[...])
    key_row = lax.broadcasted_iota(jnp.int32, (N_KEYS, tm), 0).astype(F32)
    rank_row = lax.broadcasted_iota(jnp.int32, (PK_TOPK, tm), 0)
    neg = -jnp.inf
    pa, pb = pa_ref[...], pb_ref[...]
    flat = jnp.concatenate([flat_ref[...]] * -(-tm // LANES), axis=1)[:, :tm]
    is_pair = flat < float(kk2)

    def pick(p, x):
        h = x.astype(BF16)
        r = x - h.astype(F32)
        m = r.astype(BF16)
        l = (r - m.astype(F32)).astype(BF16)
        return _mm(p, h) + _mm(p, m) + _mm(p, l)

    def top_keys(scores):
        def body(r, carry):
            s, vals, idxs = carry
            m = jnp.max(s, axis=0, keepdims=True)
            idx = jnp.min(jnp.where(s == m, key_row, float(N_KEYS)), axis=0, keepdims=True)
            s = jnp.where(key_row == idx, neg, s)
            hit = rank_row == r
            return s, jnp.where(hit, m, vals), jnp.where(hit, idx, idxs)
        zero = jnp.zeros((PK_TOPK, tm), F32)
        _, vals, idxs = lax.fori_loop(0, PK_TOPK, body, (scores, zero, zero))
        return vals, idxs

    experts, gates = [], []
    for h in range(PEER_HEADS):
        q1 = q[:, h * PK_DIM:h * PK_DIM + PK_HALF].astype(BF16)
        q2 = q[:, h * PK_DIM + PK_HALF:(h + 1) * PK_DIM].astype(BF16)
        v1, i1 = top_keys(_nt(k1_ref[h], q1))
        v2, i2 = top_keys(_nt(k2_ref[h], q2))
        cand = jnp.where(is_pair, pick(pa, v1) + pick(pb, v2), neg)
        expert = pick(pa, i1) * float(N_KEYS) + pick(pb, i2)

        def body(r, carry, expert=expert):
            cnd, top, den, e_a, p_a = carry
            m = jnp.max(cnd, axis=0, keepdims=True)
            idx = jnp.min(jnp.where(cnd == m, flat, float(kk2)), axis=0, keepdims=True)
            hit = flat == idx
            e = jnp.max(jnp.where(hit, expert, -1.0), axis=0, keepdims=True)
            cnd = jnp.where(hit, neg, cnd)
            top = jnp.where(r == 0, m, top)
            pr = jnp.exp(m - top)
            sel = rank_row == r
            return cnd, top, den + pr, jnp.where(sel, e, e_a), jnp.where(sel, pr, p_a)

        zrow = jnp.zeros((1, tm), F32)
        zk = jnp.zeros((PK_TOPK, tm), F32)
        _, _, den, e_h, p_h = lax.fori_loop(0, PK_TOPK, body, (cand, zrow, zrow, zk, zk))
        experts.append(e_h)
        gates.append(p_h / den)
    e_int = jnp.concatenate(experts, axis=0).T.astype(jnp.int32)
    i1_o[...] = e_int >> KEY_BITS
    i2_o[...] = e_int & (N_KEYS - 1)
    g_o[...] = jnp.concatenate(gates, axis=0).T


def _candidate_pairs():
    pairs = [(a, b) for a in range(PK_TOPK) for b in range(PK_TOPK // (a + 1))]
    rows = -(-len(pairs) // 16) * 16
    pa = np.zeros((rows, PK_TOPK), np.float32)
    pb = np.zeros((rows, PK_TOPK), np.float32)
    flat = np.full((rows, LANES), float(PK_TOPK * PK_TOPK), np.float32)
    for i, (a, b) in enumerate(pairs):
        pa[i, a] = 1.0
        pb[i, b] = 1.0
        flat[i, :] = a * PK_TOPK + b
    return jnp.asarray(pa, BF16), jnp.asarray(pb, BF16), jnp.asarray(flat)


def _peer_route(x2d, wq, k1, k2, tm):
    n = x2d.shape[0]
    slots = PEER_HEADS * PK_TOPK
    row = lambda c: pl.BlockSpec((tm, c), lambda i: (i, 0))
    pa, pb, flat = _candidate_pairs()
    return pl.pallas_call(
        _peer_route_kernel, grid=(n // tm,),
        in_specs=[row(D_MODEL), _resident(wq.shape), _resident(k1.shape), _resident(k2.shape),
                  _resident(pa.shape), _resident(pb.shape), _resident(flat.shape)],
        out_specs=[row(slots)] * 3,
        out_shape=[jax.ShapeDtypeStruct((n, slots), jnp.int32), jax.ShapeDtypeStruct((n, slots), jnp.int32),
                   jax.ShapeDtypeStruct((n, slots), F32)],
        compiler_params=_cparams(("parallel",)), name="peer_route",
    )(x2d, wq, k1, k2, pa, pb, flat)


def _peer_weights_kernel(i1_ref, i2_ref, g_ref, wt_ref):
    tm = i1_ref.shape[0]
    key = lax.broadcasted_iota(jnp.int32, (N_KEYS, PEER_HEADS * PK_TOPK), 0)

    def body(grp, carry):
        t0 = pl.multiple_of(grp * SUBLANES, SUBLANES)
        i1s = i1_ref[pl.ds(t0, SUBLANES), :]
        i2s = i2_ref[pl.ds(t0, SUBLANES), :]
        gs = g_ref[pl.ds(t0, SUBLANES), :]
        per_token = []
        for j in range(SUBLANES):
            g1 = jnp.where(key == i1s[j:j + 1, :], gs[j:j + 1, :], 0.0)
            hi = g1.astype(BF16)
            lo = (g1 - hi.astype(F32)).astype(BF16)
            g2 = jnp.where(key == i2s[j:j + 1, :], 1.0, 0.0).astype(BF16)
            per_token.append(_nt(jnp.concatenate([hi, lo], axis=1), jnp.concatenate([g2, g2], axis=1)))
        wt_ref[grp] = pltpu.einshape("tab->atb", jnp.stack(per_token, axis=0))
        return carry

    lax.fori_loop(0, tm // SUBLANES, body, 0)


def _peer_weights(i1, i2, g, tm):
    n, slots = i1.shape
    row = pl.BlockSpec((tm, slots), lambda i: (i, 0))
    return pl.pallas_call(
        _peer_weights_kernel, grid=(n // tm,),
        in_specs=[row, row, row],
        out_specs=pl.BlockSpec((tm // SUBLANES, N_KEYS, SUBLANES, N_KEYS), lambda i: (i, 0, 0, 0)),
        out_shape=jax.ShapeDtypeStruct((n // SUBLANES, N_KEYS, SUBLANES, N_KEYS), F32),
        compiler_params=_cparams(("parallel",)), name="peer_weights",
    )(i1, i2, g)


def _peer_dense_kernel(x_ref, wt_ref, u_ref, v_ref, y_ref, xb_ref, z_ref, *, rows_per_step):
    c = pl.program_id(1)
    tt = x_ref.shape[0]

    @pl.when(c == 0)
    def _():
        xb_ref[...] = x_ref[...].astype(BF16)
        y_ref[...] = jnp.zeros(y_ref.shape, F32)

    hid = _nt(xb_ref[...], u_ref[...])
    for a in range(rows_per_step):
        h = hid[:, a * N_KEYS:(a + 1) * N_KEYS]
        w = wt_ref[:, a].reshape(tt, N_KEYS)
        gelu = 0.5 * h * (1.0 + lax.erf(h * math.sqrt(0.5)))
        z_ref[:, a * N_KEYS:(a + 1) * N_KEYS] = (w * gelu).astype(BF16)
    y_ref[...] += _mm(z_ref[...], v_ref[...])


def _peer_dense(x2d, wt, u, v, tt, rows_per_step):
    n = x2d.shape[0]
    ne = rows_per_step * N_KEYS
    return pl.pallas_call(
        functools.partial(_peer_dense_kernel, rows_per_step=rows_per_step),
        grid=(n // tt, N_KEYS // rows_per_step),
        in_specs=[pl.BlockSpec((tt, D_MODEL), lambda i, c: (i, 0)),
                  pl.BlockSpec((tt // SUBLANES, rows_per_step, SUBLANES, N_KEYS), lambda i, c: (i, c, 0, 0)),
                  pl.BlockSpec((ne, D_MODEL), lambda i, c: (c, 0)),
                  pl.BlockSpec((ne, D_MODEL), lambda i, c: (c, 0))],
        out_specs=pl.BlockSpec((tt, D_MODEL), lambda i, c: (i, 0)),
        out_shape=jax.ShapeDtypeStruct((n, D_MODEL), F32),
        scratch_shapes=[pltpu.VMEM((tt, D_MODEL), BF16), pltpu.VMEM((tt, ne), BF16)],
        compiler_params=_cparams(("parallel", "arbitrary")), name="peer_dense",
    )(x2d, wt, u, v)


def _ple_kernel(x1_ref, y_ref, p_ref, l2g_ref, l2b_ref, wg_ref, wp_ref, l3g_ref, l3b_ref, o_ref):
    x2 = _layer_norm(ALPHA * x1_ref[...] + y_ref[...], l2g_ref[...], l2b_ref[...])
    gate = _sigmoid(_mm(x2.astype(BF16), wg_ref[...]))
    ple = _mm(p_ref[...].astype(BF16), wp_ref[...]) * gate
    o_ref[...] = _layer_norm(ALPHA * x2 + ple, l3g_ref[...], l3b_ref[...])


def _ple(x1, y, p_emb, l2g, l2b, wg, wp, l3g, l3b, tm):
    n = x1.shape[0]
    row = lambda c: pl.BlockSpec((tm, c), lambda i: (i, 0))
    vec = _resident((1, D_MODEL))
    return pl.pallas_call(
        _ple_kernel, grid=(n // tm,),
        in_specs=[row(D_MODEL), row(D_MODEL), row(PLE_DIM), vec, vec, _resident(wg.shape), _resident(wp.shape), vec, vec],
        out_specs=row(D_MODEL),
        out_shape=jax.ShapeDtypeStruct((n, D_MODEL), F32),
        compiler_params=_cparams(("parallel",)), name="ple_ln2_ln3",
    )(x1, y, p_emb, l2g, l2b, wg, wp, l3g, l3b)


def _rope_tables(pos):
    inv = ROPE_THETA ** (-jnp.arange(0, QK_ROPE, 2, dtype=F32) / QK_ROPE)
    ang = pos.astype(F32)[:, None] * inv[None, :]
    cos, sin = jnp.cos(ang), jnp.sin(ang)
    ck = jnp.concatenate([cos, cos], axis=-1)
    sk = jnp.concatenate([-sin, sin], axis=-1)
    return jnp.tile(ck, (1, MLA_HEADS)), jnp.tile(sk, (1, MLA_HEADS)), ck, sk


def _prepare_weights(w_in, kv_norm_g, w_uk, w_uv, rw_mu, rw_w0, rw_w_up, rw_a0, rw_a_up, rw_g_up,
                     rw_k_k, rw_k_a, rw_r_k, rw_gn_g, rw_gn_b, w_o, ln1_g, ln1_b,
                     peer_wq, peer_k1, peer_k2, peer_u, peer_v, ln2_g, ln2_b,
                     ple_w, ple_gate_w, ln3_g, ln3_b):
    half = QK_ROPE // 2
    swap = lambda w: jnp.concatenate([w[..., half:], w[..., :half]], axis=-1)
    wq = w_in[:, :Q_COLS].reshape(D_MODEL, MLA_HEADS, QK_NOPE + QK_ROPE)
    wq_n = wq[:, :, :QK_NOPE].reshape(D_MODEL, -1)
    wq_r = wq[:, :, QK_NOPE:]
    w_c = w_in[:, Q_COLS:Q_COLS + KV_RANK]
    w_kr = w_in[:, Q_COLS + KV_RANK:MLA_COLS]
    w_mla = jnp.concatenate([wq_n, wq_r.reshape(D_MODEL, -1), swap(wq_r).reshape(D_MODEL, -1),
                             w_c, w_kr, swap(w_kr)], axis=1).astype(BF16)
    row = lambda a: a.reshape(1, -1)
    head_of_lane = jnp.arange(RW_WIDTH) // RW_HEAD
    e = (head_of_lane[:, None] == jnp.arange(LANES)[None, :]).astype(BF16)
    half_of = (jnp.arange(2 * LANES) % LANES) // RW_HEAD
    e2 = (half_of[:, None] == half_of[None, :LANES]).astype(BF16)
    return dict(
        e2=e2,
        w_mla=w_mla, kv_g=row(kv_norm_g),
        w_uk2=w_uk.reshape(KV_RANK, -1).astype(BF16), w_uv2=w_uv.reshape(KV_RANK, -1).astype(BF16),
        w_ukt=jnp.transpose(w_uk, (1, 2, 0)).astype(BF16),
        rw=(w_in[:, MLA_COLS:].astype(BF16), row(rw_mu), row(rw_w0), rw_w_up.astype(BF16), row(rw_a0),
            rw_a_up.astype(BF16), rw_g_up.astype(BF16), row(rw_k_k), row(rw_k_a), row(rw_r_k), e, e.T),
        gn_g=row(rw_gn_g), gn_b=row(rw_gn_b), e=e, et=e.T,
        wo1=w_o[:MLA_HEADS * V_HEAD].astype(BF16), wo2=w_o[MLA_HEADS * V_HEAD:].astype(BF16),
        ln1_g=row(ln1_g), ln1_b=row(ln1_b),
        peer_wq=peer_wq.astype(BF16), peer_k1=peer_k1.astype(BF16), peer_k2=peer_k2.astype(BF16),
        peer_u=peer_u.astype(BF16), peer_v=peer_v.astype(BF16),
        ln2_g=row(ln2_g), ln2_b=row(ln2_b), ple_w=ple_w.astype(BF16), ple_gate_w=ple_gate_w.astype(BF16),
        ln3_g=row(ln3_g), ln3_b=row(ln3_b))


def _tile(n, pref):
    t = min(n, pref)
    while n % t:
        t -= 8
    return t


def _channel_mix(x2d, o_mla, raw, g, bonus, p_emb, w):
    n = x2d.shape[0]
    x1 = _mix_out(x2d, o_mla, raw, g, bonus, w["gn_g"], w["gn_b"], w["e"], w["et"], w["wo1"], w["wo2"],
                  w["ln1_g"], w["ln1_b"], _tile(n, 256))
    i1, i2, gate = _peer_route(x1, w["peer_wq"], w["peer_k1"], w["peer_k2"], _tile(n, 256))
    wt = _peer_weights(i1, i2, gate, _tile(n, 128))
    y = _peer_dense(x1, wt, w["peer_u"], w["peer_v"], _tile(n, 512), 8)
    return _ple(x1, y, p_emb, w["ln2_g"], w["ln2_b"], w["ple_gate_w"], w["ple_w"], w["ln3_g"], w["ln3_b"],
                _tile(n, 256))


def _prompt_layer(x, p_emb, w):
    b, t, _ = x.shape
    n = b * t
    x2d = x.reshape(n, D_MODEL)
    cq, sq, ck, sk = (jnp.tile(a, (b, 1)) for a in _rope_tables(jnp.arange(t)))
    qn, qr, ckv, kr, kn, v = _mla_proj(x2d, w["w_mla"], w["kv_g"], cq, sq, ck, sk,
                                       (w["w_uk2"], w["w_uv2"]), True, _tile(n, 256))
    r3 = lambda a: a.reshape(b, t, -1)
    o_mla = _mla_prompt_attention(r3(qn), r3(qr), r3(kn), r3(kr), r3(v), _tile(t, 512), _tile(t, 512))
    r, dec, k, vv, nkk, bb, g, bonus, last = _rwkv_proj_prompt(x, w["rw"], _tile(t, 256))
    s0 = jnp.zeros((b, RW_HEADS, RW_HEAD, RW_HEAD), F32)
    raw, wkv = _rwkv_scan(s0, r, dec, k, vv, nkk, bb, w["e2"], b, _tile(t, 128))
    f2 = lambda a: a.reshape(n, -1)
    y = _channel_mix(x2d, f2(o_mla), f2(raw), f2(g), f2(bonus), p_emb.reshape(n, -1), w)
    return y.reshape(b, t, D_MODEL), r3(ckv), r3(kr), wkv, last.reshape(b, RW_COLS)


def _sample_layer(x, p_emb, past_len, cache_ckv, cache_krope, page_table, wkv0, shift0, w):
    db, t, _ = x.shape
    assert t == 1, "the sample path handles one new token per sequence"
    x2d = x.reshape(db, D_MODEL)
    cq, sq, ck, sk = (jnp.tile(a, (db, 1)) for a in _rope_tables(jnp.full((1,), past_len)))
    ql, qr, ckv, kr = _mla_proj(x2d, w["w_mla"], w["kv_g"], cq, sq, ck, sk, (w["w_ukt"],), False, _tile(db, 128))
    o_lat = _mla_sample_attention(page_table, ql.reshape(db, MLA_HEADS, KV_RANK), qr.reshape(db, MLA_HEADS, QK_ROPE),
                                  ckv.reshape(db, 1, KV_RANK), kr.reshape(db, 1, QK_ROPE),
                                  cache_ckv, jnp.swapaxes(cache_krope, 1, 2),
                                  math.gcd(page_table.shape[1], DECODE_PAGES))
    o_mla = _head_up(o_lat.reshape(db, MLA_HEADS * KV_RANK), w["w_uv2"])
    r, dec, k, vv, nkk, bb, g, bonus, last = _rwkv_proj_sample(x2d, shift0, w["rw"])
    s3 = lambda a: a.reshape(db, 1, RW_WIDTH)
    raw, wkv = _rwkv_scan(wkv0, s3(r), s3(dec), s3(k), s3(vv), s3(nkk), s3(bb), w["e2"], _tile(db, 8), 1)
    y = _channel_mix(x2d, o_mla, raw.reshape(db, RW_WIDTH), g, bonus, p_emb.reshape(db, -1), w)
    return (y.reshape(db, 1, D_MODEL), ckv.reshape(db, 1, KV_RANK), kr.reshape(db, 1, QK_ROPE), wkv, last)


def kernel(x_prompt, x_sample, p_prompt, p_sample, cache_ckv, cache_krope, state_wkv, state_shift, page_table, w_in, kv_norm_g, w_uk, w_uv, rw_mu, rw_w0, rw_w_up, rw_a0, rw_a_up, rw_g_up, rw_k_k, rw_k_a, rw_r_k, rw_gn_g, rw_gn_b, w_o, ln1_g, ln1_b, peer_wq, peer_k1, peer_k2, peer_u, peer_v, ln2_g, ln2_b, ple_w, ple_gate_w, ln3_g, ln3_b):
    layer_params = (w_in, kv_norm_g, w_uk, w_uv, rw_mu, rw_w0, rw_w_up, rw_a0, rw_a_up, rw_g_up,
                    rw_k_k, rw_k_a, rw_r_k, rw_gn_g, rw_gn_b, w_o, ln1_g, ln1_b,
                    peer_wq, peer_k1, peer_k2, peer_u, peer_v, ln2_g, ln2_b,
                    ple_w, ple_gate_w, ln3_g, ln3_b)
    depth = w_in.shape[0]
    past_len = page_table.shape[1] * cache_ckv.shape[2]
    xp, xs = x_prompt, x_sample
    outs_p, outs_s = [], []
    for i in range(depth):
        w = _prepare_weights(*(a[i] for a in layer_params))
        xp, *rest_p = _prompt_layer(xp, p_prompt[i], w)
        xs, *rest_s = _sample_layer(xs, p_sample[i], past_len, cache_ckv[i], cache_krope[i], page_table,
                                    state_wkv[i], state_shift[i], w)
        outs_p.append(rest_p)
        outs_s.append(rest_s)
    stack = lambda outs, j: jnp.stack([o[j] for o in outs])
    return (xp, xs, stack(outs_p, 0), stack(outs_p, 1), stack(outs_p, 2), stack(outs_p, 3),
            stack(outs_s, 0), stack(outs_s, 1), stack(outs_s, 2), stack(outs_s, 3))
```

```python
import functools
import math

import jax
import jax.numpy as jnp
import numpy as np
from jax import lax
from jax.experimental import pallas as pl
from jax.experimental.pallas import tpu as pltpu

F32 = jnp.float32
BF16 = jnp.bfloat16

D_MODEL = 2048
MLA_HEADS = 8
QK_NOPE = 128
QK_ROPE = 64
QK_PAD = 256
V_HEAD = 128
KV_RANK = 512
ROPE_THETA = 10000.0
RW_HEADS = 16
RW_HEAD = 64
RW_WIDTH = RW_HEADS * RW_HEAD
DECAY_LORA = 64
AAA_LORA = 64
GATE_LORA = 160
RW_COLS = 3 * RW_WIDTH + DECAY_LORA + AAA_LORA + GATE_LORA
Q_COLS = MLA_HEADS * (QK_NOPE + QK_ROPE)
MLA_COLS = Q_COLS + KV_RANK + QK_ROPE
PEER_HEADS = 8
N_KEYS = 128
PK_DIM = 256
PK_HALF = PK_DIM // 2
PK_TOPK = 16
KEY_BITS = 7
TOPK_BITS = 4
PLE_DIM = 256
DEPTH = 1
ALPHA = (2 * DEPTH) ** 0.25
LN_EPS = 1e-5
RMS_EPS = 1e-6
GN_EPS = 64e-5
ATTN_SCALE = (QK_NOPE + QK_ROPE) ** -0.5

LANES = 128
SUBLANES = 8
RW_PAIRS = RW_HEADS // 2
DECODE_PAGES = 32
SCAN_DOT_PARTS = 4
SCAN_GROUP = SUBLANES
VMEM_LIMIT = 56 * 1024 * 1024

NT_DIMS = (((1,), (1,)), ((), ()))


def _cparams(sem, vmem=VMEM_LIMIT):
    return pltpu.CompilerParams(dimension_semantics=sem, vmem_limit_bytes=vmem)


def _resident(shape):
    nd = len(shape)
    return pl.BlockSpec(shape, lambda *_: (0,) * nd, pipeline_mode=pl.Buffered(1))


def _nt(a, b):
    return lax.dot_general(a, b, NT_DIMS, preferred_element_type=F32)


def _mm(a, b):
    return jnp.dot(a, b, preferred_element_type=F32)


def _dot01(x, e):
    h = x.astype(BF16)
    r = x - h.astype(F32)
    m = r.astype(BF16)
    l = (r - m.astype(F32)).astype(BF16)
    return _mm(h, e) + _mm(m, e) + _mm(l, e)


def _head_sum(x, e, et):
    return _dot01(_dot01(x, e), et)


def _sigmoid(x):
    return 1.0 / (1.0 + jnp.exp(-x))


def _layer_norm(x, g, b):
    mu = jnp.mean(x, axis=-1, keepdims=True)
    d = x - mu
    var = jnp.mean(d * d, axis=-1, keepdims=True)
    return d * lax.rsqrt(var + LN_EPS) * g + b


def _mla_proj_common(x_ref, w_ref, g_ref, cq_ref, sq_ref, ck_ref, sk_ref):
    x = x_ref[...].astype(BF16)
    res = _mm(x, w_ref[...])
    nq = MLA_HEADS * QK_NOPE
    rq = MLA_HEADS * QK_ROPE
    qn = res[:, :nq]
    qr = res[:, nq:nq + rq] * cq_ref[...] + res[:, nq + rq:nq + 2 * rq] * sq_ref[...]
    o = nq + 2 * rq
    c = res[:, o:o + KV_RANK]
    ckv = c * lax.rsqrt(jnp.mean(c * c, axis=-1, keepdims=True) + RMS_EPS) * g_ref[...]
    o += KV_RANK
    kr = res[:, o:o + QK_ROPE] * ck_ref[...] + res[:, o + QK_ROPE:o + 2 * QK_ROPE] * sk_ref[...]
    return qn, qr, ckv, kr


def _mla_proj_prompt_kernel(x_ref, w_ref, g_ref, cq_ref, sq_ref, ck_ref, sk_ref, wuk_ref, wuv_ref,
                            qc_o, kc_o, ckv_o, kr_o, v_o):
    qn, qr, ckv, kr = _mla_proj_common(x_ref, w_ref, g_ref, cq_ref, sq_ref, ck_ref, sk_ref)
    ckv_o[...] = ckv
    kr_o[...] = kr
    cb = ckv.astype(BF16)
    kn = _mm(cb, wuk_ref[...])
    v_o[...] = _mm(cb, wuv_ref[...]).astype(BF16)
    tm = qn.shape[0]
    low = lax.broadcasted_iota(jnp.int32, (tm, LANES), 1) < QK_ROPE
    kr_pad = jnp.concatenate([kr, jnp.zeros_like(kr)], axis=1).astype(BF16)
    for h in range(MLA_HEADS):
        blk = qr[:, (h // 2) * LANES:(h // 2 + 1) * LANES]
        if h % 2:
            blk = pltpu.roll(blk, QK_ROPE, axis=1)
        o = h * QK_PAD
        qc_o[:, o:o + QK_NOPE] = qn[:, h * QK_NOPE:(h + 1) * QK_NOPE].astype(BF16)
        qc_o[:, o + QK_NOPE:o + QK_PAD] = jnp.where(low, blk, 0.0).astype(BF16)
        kc_o[:, o:o + QK_NOPE] = kn[:, h * QK_NOPE:(h + 1) * QK_NOPE].astype(BF16)
        kc_o[:, o + QK_NOPE:o + QK_PAD] = kr_pad


def _mla_proj_sample_kernel(x_ref, w_ref, g_ref, cq_ref, sq_ref, ck_ref, sk_ref, wukt_ref,
                            ql_o, qr_o, ckv_o, kr_o):
    qn, qr, ckv, kr = _mla_proj_common(x_ref, w_ref, g_ref, cq_ref, sq_ref, ck_ref, sk_ref)
    qr_o[...] = qr.astype(BF16)
    ckv_o[...] = ckv
    kr_o[...] = kr
    for h in range(MLA_HEADS):
        qh = qn[:, h * QK_NOPE:(h + 1) * QK_NOPE].astype(BF16)
        ql_o[:, h * KV_RANK:(h + 1) * KV_RANK] = _mm(qh, wukt_ref[h]).astype(BF16)


def _mla_proj(x2d, w_all, kv_g, cq, sq, ck, sk, up_weights, prompt, tm):
    n = x2d.shape[0]
    wcols = w_all.shape[1]
    nq, rq = MLA_HEADS * QK_NOPE, MLA_HEADS * QK_ROPE
    row = lambda c: pl.BlockSpec((tm, c), lambda i: (i, 0))
    in_specs = [row(D_MODEL), _resident((D_MODEL, wcols)), _resident((1, KV_RANK)),
                row(rq), row(rq), row(QK_ROPE), row(QK_ROPE)]
    if prompt:
        kern = _mla_proj_prompt_kernel
        in_specs += [_resident((KV_RANK, nq)), _resident((KV_RANK, nq))]
        pad = MLA_HEADS * QK_PAD
        out_shape = [jax.ShapeDtypeStruct((n, pad), BF16), jax.ShapeDtypeStruct((n, pad), BF16),
                     jax.ShapeDtypeStruct((n, KV_RANK), F32), jax.ShapeDtypeStruct((n, QK_ROPE), F32),
                     jax.ShapeDtypeStruct((n, nq), BF16)]
        out_specs = [row(pad), row(pad), row(KV_RANK), row(QK_ROPE), row(nq)]
    else:
        kern = _mla_proj_sample_kernel
        in_specs += [_resident((MLA_HEADS, QK_NOPE, KV_RANK))]
        out_shape = [jax.ShapeDtypeStruct((n, MLA_HEADS * KV_RANK), BF16), jax.ShapeDtypeStruct((n, rq), BF16),
                     jax.ShapeDtypeStruct((n, KV_RANK), F32), jax.ShapeDtypeStruct((n, QK_ROPE), F32)]
        out_specs = [row(MLA_HEADS * KV_RANK), row(rq), row(KV_RANK), row(QK_ROPE)]
    return pl.pallas_call(
        kern, grid=(n // tm,), in_specs=in_specs, out_specs=out_specs, out_shape=out_shape,
        compiler_params=_cparams(("parallel",)), name="mla_proj",
    )(x2d, w_all, kv_g, cq, sq, ck, sk, *up_weights)


def _flash_kernel(q_ref, k_ref, v_ref, o_ref, m_ref, l_ref, acc_ref, s_ref, p_ref, *, tq, tk, nk):
    i = pl.program_id(1)
    j = pl.program_id(2)

    @pl.when(j == 0)
    def _():
        m_ref[...] = jnp.full(m_ref.shape, -jnp.inf, F32)
        l_ref[...] = jnp.zeros(l_ref.shape, F32)
        acc_ref[...] = jnp.zeros(acc_ref.shape, F32)

    def sweep(masked):
        if masked:
            qpos = i * tq + lax.broadcasted_iota(jnp.int32, (tq, tk), 0)
            kpos = j * tk + lax.broadcasted_iota(jnp.int32, (tq, tk), 1)
            visible = kpos <= qpos
        for h in range(MLA_HEADS):
            hs = slice(h * QK_PAD, (h + 1) * QK_PAD)
            s = _nt(q_ref[0, :, hs], k_ref[0, :, hs]) * ATTN_SCALE
            s_ref[h] = jnp.where(visible, s, -jnp.inf) if masked else s
        alphas = []
        for h in range(MLA_HEADS):
            s = s_ref[h]
            m_prev = m_ref[h]
            m_new = jnp.maximum(m_prev, jnp.max(s, axis=1, keepdims=True))
            alpha = jnp.exp(m_prev - m_new)
            p = jnp.exp(s - m_new)
            l_ref[h] = alpha * l_ref[h] + jnp.sum(p, axis=1, keepdims=True)
            m_ref[h] = m_new
            p_ref[h] = p.astype(BF16)
            alphas.append(alpha)
        for h in range(MLA_HEADS):
            acc_ref[h] = alphas[h] * acc_ref[h] + _mm(p_ref[h], v_ref[0, :, h * V_HEAD:(h + 1) * V_HEAD])

    first_q, last_q = i * tq, i * tq + tq - 1
    first_k, last_k = j * tk, j * tk + tk - 1

    @pl.when(last_k <= first_q)
    def _():
        sweep(False)

    @pl.when((last_k > first_q) & (first_k <= last_q))
    def _():
        sweep(True)

    @pl.when(j == nk - 1)
    def _():
        for h in range(MLA_HEADS):
            o_ref[0, :, h * V_HEAD:(h + 1) * V_HEAD] = (acc_ref[h] / l_ref[h]).astype(o_ref.dtype)


def _mla_prompt_attention(q, k, v, tq, tk):
    b, s, _ = q.shape
    nq, nk = s // tq, s // tk
    last = lambda i: (i * tq + tq - 1) // tk
    qspec = lambda c: pl.BlockSpec((1, tq, c), lambda bb, i, j: (bb, i, 0))
    kspec = lambda c: pl.BlockSpec((1, tk, c), lambda bb, i, j: (bb, jnp.minimum(j, last(i)), 0))
    return pl.pallas_call(
        functools.partial(_flash_kernel, tq=tq, tk=tk, nk=nk),
        grid=(b, nq, nk),
        in_specs=[qspec(q.shape[2]), kspec(k.shape[2]), kspec(v.shape[2])],
        out_specs=qspec(v.shape[2]),
        out_shape=jax.ShapeDtypeStruct(v.shape, BF16),
        scratch_shapes=[pltpu.VMEM((MLA_HEADS, tq, 1), F32), pltpu.VMEM((MLA_HEADS, tq, 1), F32),
                        pltpu.VMEM((MLA_HEADS, tq, V_HEAD), F32),
                        pltpu.VMEM((MLA_HEADS, tq, tk), F32), pltpu.VMEM((MLA_HEADS, tq, tk), BF16)],
        compiler_params=_cparams(("parallel", "parallel", "arbitrary")), name="mla_prompt_attention",
    )(q, k, v)


def _decode_kernel(pt_ref, ql_ref, qr_ref, cn_ref, kn_ref, *rest, pages, ng):
    del pt_ref
    c_refs, k_refs = rest[:pages], rest[pages:2 * pages]
    o_ref, m_ref, l_ref, acc_ref = rest[2 * pages:]
    _decode_init(m_ref, l_ref, acc_ref)
    _decode_sweep(ql_ref, qr_ref, c_refs, k_refs, m_ref, l_ref, acc_ref)
    _decode_final(ql_ref, qr_ref, cn_ref, kn_ref, o_ref, m_ref, l_ref, acc_ref, ng)


def _decode_init(m_ref, l_ref, acc_ref):
    @pl.when(pl.program_id(1) == 0)
    def _():
        m_ref[...] = jnp.full(m_ref.shape, -jnp.inf, F32)
        l_ref[...] = jnp.zeros(l_ref.shape, F32)
        acc_ref[...] = jnp.zeros(acc_ref.shape, F32)


def _decode_sweep(ql_ref, qr_ref, c_refs, k_refs, m_ref, l_ref, acc_ref):
    pages = len(c_refs)
    ql = ql_ref[0]
    qr = qr_ref[0]
    cs, ss = [], []
    for i in range(pages):
        c = c_refs[i][...].astype(BF16)
        k = k_refs[i][...].astype(BF16)
        cs.append(c)
        ss.append(_nt(ql, c) + _mm(qr, k))
    s = jnp.concatenate(ss, axis=1) * ATTN_SCALE
    m_prev = m_ref[...]
    m_new = jnp.maximum(m_prev, jnp.max(s, axis=1, keepdims=True))
    alpha = jnp.exp(m_prev - m_new)
    p = jnp.exp(s - m_new)
    pb = p.astype(BF16)
    page = cs[0].shape[0]
    pv = _mm(pb[:, :page], cs[0])
    for i in range(1, pages):
        pv = pv + _mm(pb[:, i * page:(i + 1) * page], cs[i])
    m_ref[...] = m_new
    l_ref[...] = alpha * l_ref[...] + jnp.sum(p, axis=1, keepdims=True)
    acc_ref[...] = alpha * acc_ref[...] + pv


def _decode_final(ql_ref, qr_ref, cn_ref, kn_ref, o_ref, m_ref, l_ref, acc_ref, ng):
    @pl.when(pl.program_id(1) == ng - 1)
    def _():
        ql, qr = ql_ref[0], qr_ref[0]
        m_new = m_ref[...]
        cn = cn_ref[0].astype(BF16).astype(F32)
        kn = kn_ref[0].astype(BF16).astype(F32)
        s_self = (jnp.sum(ql.astype(F32) * cn, axis=1, keepdims=True)
                  + jnp.sum(qr.astype(F32) * kn, axis=1, keepdims=True)) * ATTN_SCALE
        m_f = jnp.maximum(m_new, s_self)
        a_f = jnp.exp(m_new - m_f)
        p_self = jnp.exp(s_self - m_f)
        l_f = a_f * l_ref[...] + p_self
        acc_f = a_f * acc_ref[...] + p_self.astype(BF16).astype(F32) * cn
        o_ref[0] = acc_f / l_f


def _mla_sample_attention(page_table, q_lat, q_rope, c_new, kr_new, cache_ckv, cache_krope_t, pages):
    db, n_pages = page_table.shape
    _, page, _ = cache_ckv.shape
    ng = n_pages // pages
    pt = page_table.reshape(-1)
    per_seq = lambda shape: pl.BlockSpec((1,) + shape, lambda b, g, pt_: (b, 0, 0))

    def page_spec(shape, i):
        return pl.BlockSpec((None,) + shape, lambda b, g, pt_: (pt_[b * n_pages + g * pages + i], 0, 0))

    grid_spec = pltpu.PrefetchScalarGridSpec(
        num_scalar_prefetch=1, grid=(db, ng),
        in_specs=[per_seq((MLA_HEADS, KV_RANK)), per_seq((MLA_HEADS, QK_ROPE)),
                  per_seq((1, KV_RANK)), per_seq((1, QK_ROPE))]
                 + [page_spec((page, KV_RANK), i) for i in range(pages)]
                 + [page_spec((QK_ROPE, page), i) for i in range(pages)],
        out_specs=per_seq((MLA_HEADS, KV_RANK)),
        scratch_shapes=[pltpu.VMEM((MLA_HEADS, 1), F32), pltpu.VMEM((MLA_HEADS, 1), F32),
                        pltpu.VMEM((MLA_HEADS, KV_RANK), F32)])
    return pl.pallas_call(
        functools.partial(_decode_kernel, pages=pages, ng=ng),
        grid_spec=grid_spec,
        out_shape=jax.ShapeDtypeStruct((db, MLA_HEADS, KV_RANK), F32),
        compiler_params=_cparams(("parallel", "arbitrary")), name="mla_sample_attention",
    )(pt, q_lat, q_rope, c_new, kr_new, *([cache_ckv] * pages), *([cache_krope_t] * pages))


def _head_up_kernel(o_ref, w_ref, out_ref):
    out_ref[...] = _mm(o_ref[...].astype(BF16), w_ref[...]).astype(out_ref.dtype)


def _head_up(o_lat2d, w_uv2):
    n = o_lat2d.shape[0]
    return pl.pallas_call(
        _head_up_kernel, grid=(MLA_HEADS,),
        in_specs=[pl.BlockSpec((n, KV_RANK), lambda h: (0, h)), pl.BlockSpec((KV_RANK, V_HEAD), lambda h: (0, h))],
        out_specs=pl.BlockSpec((n, V_HEAD), lambda h: (0, h)),
        out_shape=jax.ShapeDtypeStruct((n, MLA_HEADS * V_HEAD), BF16),
        compiler_params=_cparams(("parallel",)), name="mla_head_up",
    )(o_lat2d, w_uv2)


def _rwkv_terms(p, prev, mu_ref, w0_ref, wup_ref, a0_ref, aup_ref, gup_ref, kk_ref, ka_ref, rk_ref,
                e_ref, et_ref, outs):
    r_o, w_o, k_o, v_o, n_o, b_o, g_o, bonus_o = outs
    xs = p + mu_ref[...] * (prev - p)
    w3 = 3 * RW_WIDTH
    r = xs[:, :RW_WIDTH]
    k0 = xs[:, RW_WIDTH:2 * RW_WIDTH]
    v = xs[:, 2 * RW_WIDTH:w3]
    xw = xs[:, w3:w3 + DECAY_LORA]
    xa = xs[:, w3 + DECAY_LORA:w3 + DECAY_LORA + AAA_LORA]
    xg = xs[:, w3 + DECAY_LORA + AAA_LORA:]
    y = w0_ref[...] + _mm(jnp.tanh(xw).astype(BF16), wup_ref[...])
    decay = jnp.exp(-math.exp(-0.5) * _sigmoid(y))
    a = _sigmoid(a0_ref[...] + _mm(xa.astype(BF16), aup_ref[...]))
    g = _mm(_sigmoid(xg).astype(BF16), gup_ref[...])
    kk = k0 * kk_ref[...]
    e, et = e_ref[...], et_ref[...]
    kk = kk * lax.rsqrt(jnp.maximum(_head_sum(kk * kk, e, et), 1e-24))
    k = k0 * (1.0 + (a - 1.0) * ka_ref[...])
    r_o[...] = r
    w_o[...] = decay
    k_o[...] = k
    v_o[...] = v
    n_o[...] = -kk
    b_o[...] = kk * a
    g_o[...] = g
    bonus_o[...] = _head_sum(r * k * rk_ref[...], e, et) * v


def _rwkv_proj_prompt_kernel(x_ref, w_ref, mu_ref, w0_ref, wup_ref, a0_ref, aup_ref, gup_ref,
                             kk_ref, ka_ref, rk_ref, e_ref, et_ref,
                             r_o, w_o, k_o, v_o, n_o, b_o, g_o, bonus_o, last_o, carry_ref):
    tm = x_ref.shape[1]

    @pl.when(pl.program_id(1) == 0)
    def _():
        carry_ref[...] = jnp.zeros(carry_ref.shape, F32)

    p = _mm(x_ref[0].astype(BF16), w_ref[...])
    rows = lax.broadcasted_iota(jnp.int32, p.shape, 0)
    prev = jnp.where(rows == 0, carry_ref[...], pltpu.roll(p, 1, axis=0))
    last = p[tm - 1:tm, :]
    carry_ref[...] = last
    last_o[0] = last
    outs = tuple(o.at[0] for o in (r_o, w_o, k_o, v_o, n_o, b_o, g_o, bonus_o))
    _rwkv_terms(p, prev, mu_ref, w0_ref, wup_ref, a0_ref, aup_ref, gup_ref, kk_ref, ka_ref, rk_ref,
                e_ref, et_ref, outs)


def _rwkv_proj_sample_kernel(x_ref, prev_ref, w_ref, mu_ref, w0_ref, wup_ref, a0_ref, aup_ref, gup_ref,
                             kk_ref, ka_ref, rk_ref, e_ref, et_ref,
                             r_o, w_o, k_o, v_o, n_o, b_o, g_o, bonus_o, last_o):
    p = _mm(x_ref[...].astype(BF16), w_ref[...])
    last_o[...] = p
    _rwkv_terms(p, prev_ref[...], mu_ref, w0_ref, wup_ref, a0_ref, aup_ref, gup_ref, kk_ref, ka_ref, rk_ref,
                e_ref, et_ref, (r_o, w_o, k_o, v_o, n_o, b_o, g_o, bonus_o))


def _rwkv_weight_specs():
    vec = lambda c: _resident((1, c))
    return [_resident((D_MODEL, RW_COLS)), vec(RW_COLS), vec(RW_WIDTH), _resident((DECAY_LORA, RW_WIDTH)),
            vec(RW_WIDTH), _resident((AAA_LORA, RW_WIDTH)), _resident((GATE_LORA, RW_WIDTH)),
            vec(RW_WIDTH), vec(RW_WIDTH), vec(RW_WIDTH),
            _resident((RW_WIDTH, LANES)), _resident((LANES, RW_WIDTH))]


def _rwkv_proj_prompt(x, rw_weights, tm):
    b, t, _ = x.shape
    tile = lambda c: pl.BlockSpec((1, tm, c), lambda bb, i: (bb, i, 0))
    wide = jax.ShapeDtypeStruct((b, t, RW_WIDTH), F32)
    return pl.pallas_call(
        _rwkv_proj_prompt_kernel, grid=(b, t // tm),
        in_specs=[tile(D_MODEL)] + _rwkv_weight_specs(),
        out_specs=[tile(RW_WIDTH)] * 8 + [pl.BlockSpec((1, 1, RW_COLS), lambda bb, i: (bb, 0, 0))],
        out_shape=[wide] * 8 + [jax.ShapeDtypeStruct((b, 1, RW_COLS), F32)],
        scratch_shapes=[pltpu.VMEM((1, RW_COLS), F32)],
        compiler_params=_cparams(("parallel", "arbitrary")), name="rwkv_proj_prompt",
    )(x, *rw_weights)


def _rwkv_proj_sample(x2d, prev, rw_weights):
    n = x2d.shape[0]
    full = lambda c: pl.BlockSpec((n, c), lambda i: (0, 0))
    wide = jax.ShapeDtypeStruct((n, RW_WIDTH), F32)
    return pl.pallas_call(
        _rwkv_proj_sample_kernel, grid=(1,),
        in_specs=[full(D_MODEL), full(RW_COLS)] + _rwkv_weight_specs(),
        out_specs=[full(RW_WIDTH)] * 8 + [full(RW_COLS)],
        out_shape=[wide] * 8 + [jax.ShapeDtypeStruct((n, RW_COLS), F32)],
        compiler_params=_cparams(("arbitrary",)), name="rwkv_proj_sample",
    )(x2d, prev, *rw_weights)


def _scan_load_state(s0_ref, state_ref):
    for b in range(state_ref.shape[0]):
        for p in range(RW_PAIRS):
            state_ref[b, p] = jnp.concatenate([s0_ref[b, 2 * p], s0_ref[b, 2 * p + 1]], axis=1)


def _scan_store_state(st_ref, state_ref):
    for b in range(state_ref.shape[0]):
        for p in range(RW_PAIRS):
            s = state_ref[b, p]
            st_ref[b, 2 * p] = s[:, :RW_HEAD]
            st_ref[b, 2 * p + 1] = s[:, RW_HEAD:]


def _scan_steps(ins, e2_ref, eo_ref, vt_ref, ev_ref, out_ref, state_ref, vcol_ref, group):
    nb = state_ref.shape[0]
    chains = [(b, p) for b in range(nb) for p in range(RW_PAIRS)]
    lane = lax.broadcasted_iota(jnp.int32, (RW_HEAD, LANES), 1)
    sub = lax.broadcasted_iota(jnp.int32, (RW_HEAD, LANES), 0)
    diag = (lane & (RW_HEAD - 1)) == sub
    e2 = e2_ref[...]

    def pieces(x):
        hi = x.astype(BF16)
        return jnp.concatenate([hi, (x - hi.astype(F32)).astype(BF16)], axis=1)

    def head_sums(xs):
        out = []
        per = max(len(xs) // SCAN_DOT_PARTS, 1)
        for q in range(0, len(xs), per):
            res = _mm(jnp.concatenate([pieces(x) for x in xs[q:q + per]], axis=0), e2)
            out += [res[i * RW_HEAD:(i + 1) * RW_HEAD] for i in range(per)]
        return out

    def head_sums_out(xs):
        eo = eo_ref[...]
        out = []
        per = max(len(xs) // SCAN_DOT_PARTS, 2)
        for q in range(0, len(xs), per):
            res = _mm(jnp.concatenate([jnp.concatenate([xs[i].astype(BF16), xs[i + 1].astype(BF16)], axis=1)
                                       for i in range(q, q + per, 2)], axis=0), eo)
            for i in range(per // 2):
                blk = res[i * RW_HEAD:(i + 1) * RW_HEAD]
                out += [blk[:, :LANES], blk[:, LANES:]]
        return out

    def steps(tg):
        t0 = tg * group if isinstance(tg, int) else pl.multiple_of(tg * group, group)
        rows = [[ref[b, pl.ds(t0, group), p * LANES:(p + 1) * LANES] for ref in ins] for b, p in chains]
        outs = [[] for _ in chains]
        if group > 1:
            lhs = [pieces(jnp.where((lane % RW_HEADS) // 2 == p, vt_ref[b, tg], 0.0)) for b, p in chains]
            vcol_ref[...] = _mm(jnp.concatenate(lhs, axis=0), ev_ref[...])
        for i in range(group):
            r, w, k, v, n, bb = ([x[j][i:i + 1, :] for x in rows] for j in range(len(ins)))
            if group > 1:
                v_cols = [vcol_ref[ci * RW_HEAD:(ci + 1) * RW_HEAD, i * LANES:(i + 1) * LANES]
                          for ci in range(len(chains))]
            else:
                v_cols = head_sums([jnp.where(diag, vi, 0.0) for vi in v])
            sa = head_sums([state_ref[b, p] * n[ci] for ci, (b, p) in enumerate(chains)])
            new = []
            for ci, (b, p) in enumerate(chains):
                s = state_ref[b, p] * w[ci] + sa[ci] * bb[ci] + v_cols[ci] * k[ci]
                state_ref[b, p] = s
                new.append(s * r[ci])
            for ci, o_col in enumerate(head_sums_out(new)):
                outs[ci].append(jnp.sum(jnp.where(diag, o_col, 0.0), axis=0, keepdims=True))
        for ci, (b, p) in enumerate(chains):
            out_ref[b, pl.ds(t0, group), p * LANES:(p + 1) * LANES] = (
                outs[ci][0] if group == 1 else jnp.concatenate(outs[ci], axis=0))

    return steps


def _scan_loop(steps, n_groups):
    if n_groups == 1:
        steps(0)
    else:
        def body(tg, carry):
            steps(tg)
            return carry
        lax.fori_loop(0, n_groups, body, 0)


def _rwkv_scan_kernel(s0_ref, r_ref, w_ref, k_ref, v_ref, n_ref, b_ref, e2_ref, eo_ref, *rest, tc, nc):
    c = pl.program_id(1)
    group = SCAN_GROUP if tc % SCAN_GROUP == 0 else 1
    if group > 1:
        vt_ref, ev_ref, out_ref, st_ref, state_ref, vcol_ref = rest
    else:
        (out_ref, st_ref, state_ref), vt_ref, ev_ref, vcol_ref = rest, None, None, None

    @pl.when(c == 0)
    def _():
        _scan_load_state(s0_ref, state_ref)

    _scan_loop(_scan_steps((r_ref, w_ref, k_ref, v_ref, n_ref, b_ref), e2_ref, eo_ref, vt_ref, ev_ref,
                           out_ref, state_ref, vcol_ref, group), tc // group)

    @pl.when(c == nc - 1)
    def _():
        _scan_store_state(st_ref, state_ref)


def _scan_constants(e2):
    same_head = e2[:LANES]
    zero = jnp.zeros_like(same_head)
    eo = jnp.concatenate([jnp.concatenate([same_head, zero], axis=1),
                          jnp.concatenate([zero, same_head], axis=1)], axis=0)
    src = jnp.arange(2 * LANES) % LANES
    dst = jnp.arange(SCAN_GROUP * LANES)
    ev = ((src[:, None] // RW_HEADS == dst[None, :] // LANES)
          & (src[:, None] % 2 == (dst[None, :] % LANES) // RW_HEAD))
    return eo, ev.astype(BF16)


def _transposed_v(v):
    b, t, _ = v.shape
    return jnp.swapaxes(v.reshape(b, t // SCAN_GROUP, SCAN_GROUP * RW_HEADS, RW_HEAD), 2, 3)


def _rwkv_scan(s0, r, w, k, v, n, bb, e2, nb, tc):
    b, t, _ = r.shape
    nc = t // tc
    seq = pl.BlockSpec((nb, tc, RW_WIDTH), lambda i, c: (i, c, 0))
    st = pl.BlockSpec((nb, RW_HEADS, RW_HEAD, RW_HEAD), lambda i, c: (i, 0, 0, 0))
    eo, ev = _scan_constants(e2)
    in_specs = [st] + [seq] * 6 + [_resident(e2.shape), _resident(eo.shape)]
    args = [s0, r, w, k, v, n, bb, e2, eo]
    scratch = [pltpu.VMEM((nb, RW_PAIRS, RW_HEAD, LANES), F32)]
    if tc % SCAN_GROUP == 0:
        in_specs += [pl.BlockSpec((nb, tc // SCAN_GROUP, RW_HEAD, LANES), lambda i, c: (i, c, 0, 0)),
                     _resident(ev.shape)]
        args += [_transposed_v(v), ev]
        scratch.append(pltpu.VMEM((nb * RW_PAIRS * RW_HEAD, SCAN_GROUP * LANES), F32))
    return pl.pallas_call(
        functools.partial(_rwkv_scan_kernel, tc=tc, nc=nc),
        grid=(b // nb, nc),
        in_specs=in_specs,
        out_specs=[seq, st],
        out_shape=[jax.ShapeDtypeStruct((b, t, RW_WIDTH), F32),
                   jax.ShapeDtypeStruct((b, RW_HEADS, RW_HEAD, RW_HEAD), F32)],
        scratch_shapes=scratch,
        compiler_params=_cparams(("parallel", "arbitrary")), name="rwkv_scan",
    )(*args)


def _mix_out_kernel(x_ref, om_ref, raw_ref, g_ref, bonus_ref, gng_ref, gnb_ref, e_ref, et_ref,
                    wo1_ref, wo2_ref, lng_ref, lnb_ref, o_ref):
    e, et = e_ref[...], et_ref[...]
    raw = raw_ref[...]
    mu = _head_sum(raw, e, et) * (1.0 / RW_HEAD)
    d = raw - mu
    var = _head_sum(d * d, e, et) * (1.0 / RW_HEAD)
    o_rw = (d * lax.rsqrt(var + GN_EPS) * gng_ref[...] + gnb_ref[...] + bonus_ref[...]) * g_ref[...]
    mixed = _mm(om_ref[...], wo1_ref[...]) + _mm(o_rw.astype(BF16), wo2_ref[...])
    o_ref[...] = _layer_norm(ALPHA * x_ref[...] + mixed, lng_ref[...], lnb_ref[...])


def _mix_out(x2d, o_mla, raw, g, bonus, gng, gnb, e, et, wo1, wo2, lng, lnb, tm):
    n = x2d.shape[0]
    row = lambda c: pl.BlockSpec((tm, c), lambda i: (i, 0))
    vec = lambda c: _resident((1, c))
    return pl.pallas_call(
        _mix_out_kernel, grid=(n // tm,),
        in_specs=[row(D_MODEL), row(RW_WIDTH), row(RW_WIDTH), row(RW_WIDTH), row(RW_WIDTH),
                  vec(RW_WIDTH), vec(RW_WIDTH), _resident((RW_WIDTH, LANES)), _resident((LANES, RW_WIDTH)),
                  _resident(wo1.shape), _resident(wo2.shape), vec(D_MODEL), vec(D_MODEL)],
        out_specs=row(D_MODEL),
        out_shape=jax.ShapeDtypeStruct((n, D_MODEL), F32),
        compiler_params=_cparams(("parallel",)), name="mix_out_ln1",
    )(x2d, o_mla, raw, g, bonus, gng, gnb, e, et, wo1, wo2, lng, lnb)


def _peer_route_kernel(x_ref, wq_ref, k1_ref, k2_ref, pa_ref, pb_ref, flat_ref, i1_o, i2_o, g_o):
    tm = x_ref.shape[0]
    kk2 = PK_TOPK * PK_TOPK
    q = _mm(x_ref[...].astype(BF16), wq_ref[...])
    key_row = lax.broadcasted_iota(jnp.int32, (N_KEYS, tm), 0).astype(F32)
    rank_row = lax.broadcasted_iota(jnp.int32, (PK_TOPK, tm), 0)
    neg = -jnp.inf
    pa, pb = pa_ref[...], pb_ref[...]
    flat = jnp.concatenate([flat_ref[...]] * -(-tm // LANES), axis=1)[:, :tm]
    is_pair = flat < float(kk2)

    def pick(p, x):
        h = x.astype(BF16)
        r = x - h.astype(F32)
        m = r.astype(BF16)
        l = (r - m.astype(F32)).astype(BF16)
        return _mm(p, h) + _mm(p, m) + _mm(p, l)

    def top_keys(scores):
        def body(r, carry):
            s, vals, idxs = carry
            m = jnp.max(s, axis=0, keepdims=True)
            idx = jnp.min(jnp.where(s == m, key_row, float(N_KEYS)), axis=0, keepdims=True)
            s = jnp.where(key_row == idx, neg, s)
            hit = rank_row == r
            return s, jnp.where(hit, m, vals), jnp.where(hit, idx, idxs)
        zero = jnp.zeros((PK_TOPK, tm), F32)
        _, vals, idxs = lax.fori_loop(0, PK_TOPK, body, (scores, zero, zero))
        return vals, idxs

    experts, gates = [], []
    for h in range(PEER_HEADS):
        q1 = q[:, h * PK_DIM:h * PK_DIM + PK_HALF].astype(BF16)
        q2 = q[:, h * PK_DIM + PK_HALF:(h + 1) * PK_DIM].astype(BF16)
        v1, i1 = top_keys(_nt(k1_ref[h], q1))
        v2, i2 = top_keys(_nt(k2_ref[h], q2))
        cand = jnp.where(is_pair, pick(pa, v1) + pick(pb, v2), neg)
        expert = pick(pa, i1) * float(N_KEYS) + pick(pb, i2)

        def body(r, carry, expert=expert):
            cnd, top, den, e_a, p_a = carry
            m = jnp.max(cnd, axis=0, keepdims=True)
            idx = jnp.min(jnp.where(cnd == m, flat, float(kk2)), axis=0, keepdims=True)
            hit = flat == idx
            e = jnp.max(jnp.where(hit, expert, -1.0), axis=0, keepdims=True)
            cnd = jnp.where(hit, neg, cnd)
            top = jnp.where(r == 0, m, top)
            pr = jnp.exp(m - top)
            sel = rank_row == r
            return cnd, top, den + pr, jnp.where(sel, e, e_a), jnp.where(sel, pr, p_a)

        zrow = jnp.zeros((1, tm), F32)
        zk = jnp.zeros((PK_TOPK, tm), F32)
        _, _, den, e_h, p_h = lax.fori_loop(0, PK_TOPK, body, (cand, zrow, zrow, zk, zk))
        experts.append(e_h)
        gates.append(p_h / den)
    e_int = jnp.concatenate(experts, axis=0).T.astype(jnp.int32)
    i1_o[...] = e_int >> KEY_BITS
    i2_o[...] = e_int & (N_KEYS - 1)
    g_o[...] = jnp.concatenate(gates, axis=0).T


def _candidate_pairs():
    pairs = [(a, b) for a in range(PK_TOPK) for b in range(PK_TOPK // (a + 1))]
    rows = -(-len(pairs) // 16) * 16
    pa = np.zeros((rows, PK_TOPK), np.float32)
    pb = np.zeros((rows, PK_TOPK), np.float32)
    flat = np.full((rows, LANES), float(PK_TOPK * PK_TOPK), np.float32)
    for i, (a, b) in enumerate(pairs):
        pa[i, a] = 1.0
        pb[i, b] = 1.0
        flat[i, :] = a * PK_TOPK + b
    return jnp.asarray(pa, BF16), jnp.asarray(pb, BF16), jnp.asarray(flat)


def _peer_route(x2d, wq, k1, k2, tm):
    n = x2d.shape[0]
    slots = PEER_HEADS * PK_TOPK
    row = lambda c: pl.BlockSpec((tm, c), lambda i: (i, 0))
    pa, pb, flat = _candidate_pairs()
    return pl.pallas_call(
        _peer_route_kernel, grid=(n // tm,),
        in_specs=[row(D_MODEL), _resident(wq.shape), _resident(k1.shape), _resident(k2.shape),
                  _resident(pa.shape), _resident(pb.shape), _resident(flat.shape)],
        out_specs=[row(slots)] * 3,
        out_shape=[jax.ShapeDtypeStruct((n, slots), jnp.int32), jax.ShapeDtypeStruct((n, slots), jnp.int32),
                   jax.ShapeDtypeStruct((n, slots), F32)],
        compiler_params=_cparams(("parallel",)), name="peer_route",
    )(x2d, wq, k1, k2, pa, pb, flat)


def _peer_weights_kernel(i1_ref, i2_ref, g_ref, wt_ref):
    tm = i1_ref.shape[0]
    key = lax.broadcasted_iota(jnp.int32, (N_KEYS, PEER_HEADS * PK_TOPK), 0)

    def body(grp, carry):
        t0 = pl.multiple_of(grp * SUBLANES, SUBLANES)
        i1s = i1_ref[pl.ds(t0, SUBLANES), :]
        i2s = i2_ref[pl.ds(t0, SUBLANES), :]
        gs = g_ref[pl.ds(t0, SUBLANES), :]
        per_token = []
        for j in range(SUBLANES):
            g1 = jnp.where(key == i1s[j:j + 1, :], gs[j:j + 1, :], 0.0)
            hi = g1.astype(BF16)
            lo = (g1 - hi.astype(F32)).astype(BF16)
            g2 = jnp.where(key == i2s[j:j + 1, :], 1.0, 0.0).astype(BF16)
            per_token.append(_nt(jnp.concatenate([hi, lo], axis=1), jnp.concatenate([g2, g2], axis=1)))
        wt_ref[grp] = jnp.swapaxes(jnp.stack(per_token, axis=0), 0, 1)
        return carry

    lax.fori_loop(0, tm // SUBLANES, body, 0)


def _peer_weights(i1, i2, g, tm):
    n, slots = i1.shape
    row = pl.BlockSpec((tm, slots), lambda i: (i, 0))
    return pl.pallas_call(
        _peer_weights_kernel, grid=(n // tm,),
        in_specs=[row, row, row],
        out_specs=pl.BlockSpec((tm // SUBLANES, N_KEYS, SUBLANES, N_KEYS), lambda i: (i, 0, 0, 0)),
        out_shape=jax.ShapeDtypeStruct((n // SUBLANES, N_KEYS, SUBLANES, N_KEYS), F32),
        compiler_params=_cparams(("parallel",)), name="peer_weights",
    )(i1, i2, g)


def _peer_dense_kernel(x_ref, wt_ref, u_ref, v_ref, y_ref, xb_ref, z_ref, *, rows_per_step):
    c = pl.program_id(1)
    tt = x_ref.shape[0]

    @pl.when(c == 0)
    def _():
        xb_ref[...] = x_ref[...].astype(BF16)
        y_ref[...] = jnp.zeros(y_ref.shape, F32)

    hid = _nt(xb_ref[...], u_ref[...])
    for a in range(rows_per_step):
        h = hid[:, a * N_KEYS:(a + 1) * N_KEYS]
        w = wt_ref[:, a].reshape(tt, N_KEYS)
        gelu = 0.5 * h * (1.0 + lax.erf(h * math.sqrt(0.5)))
        z_ref[:, a * N_KEYS:(a + 1) * N_KEYS] = (w * gelu).astype(BF16)
    y_ref[...] += _mm(z_ref[...], v_ref[...])


def _peer_dense(x2d, wt, u, v, tt, rows_per_step):
    n = x2d.shape[0]
    ne = rows_per_step * N_KEYS
    return pl.pallas_call(
        functools.partial(_peer_dense_kernel, rows_per_step=rows_per_step),
        grid=(n // tt, N_KEYS // rows_per_step),
        in_specs=[pl.BlockSpec((tt, D_MODEL), lambda i, c: (i, 0)),
                  pl.BlockSpec((tt // SUBLANES, rows_per_step, SUBLANES, N_KEYS), lambda i, c: (i, c, 0, 0)),
                  pl.BlockSpec((ne, D_MODEL), lambda i, c: (c, 0)),
                  pl.BlockSpec((ne, D_MODEL), lambda i, c: (c, 0))],
        out_specs=pl.BlockSpec((tt, D_MODEL), lambda i, c: (i, 0)),
        out_shape=jax.ShapeDtypeStruct((n, D_MODEL), F32),
        scratch_shapes=[pltpu.VMEM((tt, D_MODEL), BF16), pltpu.VMEM((tt, ne), BF16)],
        compiler_params=_cparams(("parallel", "arbitrary")), name="peer_dense",
    )(x2d, wt, u, v)


def _ple_kernel(x1_ref, y_ref, p_ref, l2g_ref, l2b_ref, wg_ref, wp_ref, l3g_ref, l3b_ref, o_ref):
    x2 = _layer_norm(ALPHA * x1_ref[...] + y_ref[...], l2g_ref[...], l2b_ref[...])
    gate = _sigmoid(_mm(x2.astype(BF16), wg_ref[...]))
    ple = _mm(p_ref[...].astype(BF16), wp_ref[...]) * gate
    o_ref[...] = _layer_norm(ALPHA * x2 + ple, l3g_ref[...], l3b_ref[...])


def _ple(x1, y, p_emb, l2g, l2b, wg, wp, l3g, l3b, tm):
    n = x1.shape[0]
    row = lambda c: pl.BlockSpec((tm, c), lambda i: (i, 0))
    vec = _resident((1, D_MODEL))
    return pl.pallas_call(
        _ple_kernel, grid=(n // tm,),
        in_specs=[row(D_MODEL), row(D_MODEL), row(PLE_DIM), vec, vec, _resident(wg.shape), _resident(wp.shape), vec, vec],
        out_specs=row(D_MODEL),
        out_shape=jax.ShapeDtypeStruct((n, D_MODEL), F32),
        compiler_params=_cparams(("parallel",)), name="ple_ln2_ln3",
    )(x1, y, p_emb, l2g, l2b, wg, wp, l3g, l3b)


def _rope_tables(pos):
    inv = ROPE_THETA ** (-jnp.arange(0, QK_ROPE, 2, dtype=F32) / QK_ROPE)
    ang = pos.astype(F32)[:, None] * inv[None, :]
    cos, sin = jnp.cos(ang), jnp.sin(ang)
    ck = jnp.concatenate([cos, cos], axis=-1)
    sk = jnp.concatenate([-sin, sin], axis=-1)
    return jnp.tile(ck, (1, MLA_HEADS)), jnp.tile(sk, (1, MLA_HEADS)), ck, sk


def _prepare_weights(w_in, kv_norm_g, w_uk, w_uv, rw_mu, rw_w0, rw_w_up, rw_a0, rw_a_up, rw_g_up,
                     rw_k_k, rw_k_a, rw_r_k, rw_gn_g, rw_gn_b, w_o, ln1_g, ln1_b,
                     peer_wq, peer_k1, peer_k2, peer_u, peer_v, ln2_g, ln2_b,
                     ple_w, ple_gate_w, ln3_g, ln3_b):
    half = QK_ROPE // 2
    swap = lambda w: jnp.concatenate([w[..., half:], w[..., :half]], axis=-1)
    wq = w_in[:, :Q_COLS].reshape(D_MODEL, MLA_HEADS, QK_NOPE + QK_ROPE)
    wq_n = wq[:, :, :QK_NOPE].reshape(D_MODEL, -1)
    wq_r = wq[:, :, QK_NOPE:]
    w_c = w_in[:, Q_COLS:Q_COLS + KV_RANK]
    w_kr = w_in[:, Q_COLS + KV_RANK:MLA_COLS]
    w_mla = jnp.concatenate([wq_n, wq_r.reshape(D_MODEL, -1), swap(wq_r).reshape(D_MODEL, -1),
                             w_c, w_kr, swap(w_kr)], axis=1).astype(BF16)
    row = lambda a: a.reshape(1, -1)
    head_of_lane = jnp.arange(RW_WIDTH) // RW_HEAD
    e = (head_of_lane[:, None] == jnp.arange(LANES)[None, :]).astype(BF16)
    half_of = (jnp.arange(2 * LANES) % LANES) // RW_HEAD
    e2 = (half_of[:, None] == half_of[None, :LANES]).astype(BF16)
    return dict(
        e2=e2,
        w_mla=w_mla, kv_g=row(kv_norm_g),
        w_uk2=w_uk.reshape(KV_RANK, -1).astype(BF16), w_uv2=w_uv.reshape(KV_RANK, -1).astype(BF16),
        w_ukt=jnp.transpose(w_uk, (1, 2, 0)).astype(BF16),
        rw=(w_in[:, MLA_COLS:].astype(BF16), row(rw_mu), row(rw_w0), rw_w_up.astype(BF16), row(rw_a0),
            rw_a_up.astype(BF16), rw_g_up.astype(BF16), row(rw_k_k), row(rw_k_a), row(rw_r_k), e, e.T),
        gn_g=row(rw_gn_g), gn_b=row(rw_gn_b), e=e, et=e.T,
        wo1=w_o[:MLA_HEADS * V_HEAD].astype(BF16), wo2=w_o[MLA_HEADS * V_HEAD:].astype(BF16),
        ln1_g=row(ln1_g), ln1_b=row(ln1_b),
        peer_wq=peer_wq.astype(BF16), peer_k1=peer_k1.astype(BF16), peer_k2=peer_k2.astype(BF16),
        peer_u=peer_u.astype(BF16), peer_v=peer_v.astype(BF16),
        ln2_g=row(ln2_g), ln2_b=row(ln2_b), ple_w=ple_w.astype(BF16), ple_gate_w=ple_gate_w.astype(BF16),
        ln3_g=row(ln3_g), ln3_b=row(ln3_b))


def _tile(n, pref):
    t = min(n, pref)
    while n % t:
        t -= 8
    return t


def _channel_mix(x2d, o_mla, raw, g, bonus, p_emb, w):
    n = x2d.shape[0]
    x1 = _mix_out(x2d, o_mla, raw, g, bonus, w["gn_g"], w["gn_b"], w["e"], w["et"], w["wo1"], w["wo2"],
                  w["ln1_g"], w["ln1_b"], _tile(n, 256))
    i1, i2, gate = _peer_route(x1, w["peer_wq"], w["peer_k1"], w["peer_k2"], _tile(n, 256))
    wt = _peer_weights(i1, i2, gate, _tile(n, 128))
    y = _peer_dense(x1, wt, w["peer_u"], w["peer_v"], _tile(n, 512), 8)
    return _ple(x1, y, p_emb, w["ln2_g"], w["ln2_b"], w["ple_gate_w"], w["ple_w"], w["ln3_g"], w["ln3_b"],
                _tile(n, 256))


def _layer(xp, xs, pp, ps, past_len, cache_ckv, cache_krope, page_table, wkv0, shift0, w):
    b, t, _ = xp.shape
    db, ts, _ = xs.shape
    assert ts == 1, "the sample path handles one new token per sequence"
    n = b * t
    r3 = lambda a: a.reshape(b, t, -1)
    f2 = lambda a: a.reshape(n, -1)

    xp2 = xp.reshape(n, D_MODEL)
    cq, sq, ck, sk = (jnp.tile(a, (b, 1)) for a in _rope_tables(jnp.arange(t)))
    qc, kc, ckv_p, kr_p, v = _mla_proj(xp2, w["w_mla"], w["kv_g"], cq, sq, ck, sk,
                                       (w["w_uk2"], w["w_uv2"]), True, _tile(n, 256))
    o_mla_p = _mla_prompt_attention(r3(qc), r3(kc), r3(v), _tile(t, 512), _tile(t, 512))
    r, dec, k, vv, nkk, bb, g_p, bonus_p, last_p = _rwkv_proj_prompt(xp, w["rw"], _tile(t, 256))
    s0 = jnp.zeros((b, RW_HEADS, RW_HEAD, RW_HEAD), F32)

    xs2 = xs.reshape(db, D_MODEL)
    cq, sq, ck, sk = (jnp.tile(a, (db, 1)) for a in _rope_tables(jnp.full((1,), past_len)))
    ql, qr_s, ckv_s, kr_s = _mla_proj(xs2, w["w_mla"], w["kv_g"], cq, sq, ck, sk, (w["w_ukt"],), False,
                                      _tile(db, 128))
    decode_args = (page_table, ql.reshape(db, MLA_HEADS, KV_RANK), qr_s.reshape(db, MLA_HEADS, QK_ROPE),
                   ckv_s.reshape(db, 1, KV_RANK), kr_s.reshape(db, 1, QK_ROPE),
                   cache_ckv, jnp.swapaxes(cache_krope, 1, 2), math.gcd(page_table.shape[1], DECODE_PAGES))
    o_lat = _mla_sample_attention(*decode_args)
    raw_p, wkv_p = _rwkv_scan(s0, r, dec, k, vv, nkk, bb, w["e2"], b, _tile(t, 128))
    yp = _channel_mix(xp2, f2(o_mla_p), f2(raw_p), f2(g_p), f2(bonus_p), pp.reshape(n, -1), w)

    o_mla_s = _head_up(o_lat.reshape(db, MLA_HEADS * KV_RANK), w["w_uv2"])
    r, dec, k, vv, nkk, bb, g_s, bonus_s, last_s = _rwkv_proj_sample(xs2, shift0, w["rw"])
    s3 = lambda a: a.reshape(db, 1, RW_WIDTH)
    raw_s, wkv_s = _rwkv_scan(wkv0, s3(r), s3(dec), s3(k), s3(vv), s3(nkk), s3(bb), w["e2"], _tile(db, 8), 1)
    ys = _channel_mix(xs2, o_mla_s, raw_s.reshape(db, RW_WIDTH), g_s, bonus_s, ps.reshape(db, -1), w)
    return ((yp.reshape(b, t, D_MODEL), r3(ckv_p), r3(kr_p), wkv_p, last_p.reshape(b, RW_COLS)),
            (ys.reshape(db, 1, D_MODEL), ckv_s.reshape(db, 1, KV_RANK), kr_s.reshape(db, 1, QK_ROPE), wkv_s, last_s))


def kernel(x_prompt, x_sample, p_prompt, p_sample, cache_ckv, cache_krope, state_wkv, state_shift, page_table, w_in, kv_norm_g, w_uk, w_uv, rw_mu, rw_w0, rw_w_up, rw_a0, rw_a_up, rw_g_up, rw_k_k, rw_k_a, rw_r_k, rw_gn_g, rw_gn_b, w_o, ln1_g, ln1_b, peer_wq, peer_k1, peer_k2, peer_u, peer_v, ln2_g, ln2_b, ple_w, ple_gate_w, ln3_g, ln3_b):
    layer_params = (w_in, kv_norm_g, w_uk, w_uv, rw_mu, rw_w0, rw_w_up, rw_a0, rw_a_up, rw_g_up,
                    rw_k_k, rw_k_a, rw_r_k, rw_gn_g, rw_gn_b, w_o, ln1_g, ln1_b,
                    peer_wq, peer_k1, peer_k2, peer_u, peer_v, ln2_g, ln2_b,
                    ple_w, ple_gate_w, ln3_g, ln3_b)
    depth = w_in.shape[0]
    past_len = page_table.shape[1] * cache_ckv.shape[2]
    xp, xs = x_prompt, x_sample
    outs_p, outs_s = [], []
    for i in range(depth):
        w = _prepare_weights(*(a[i] for a in layer_params))
        (xp, *rest_p), (xs, *rest_s) = _layer(xp, xs, p_prompt[i], p_sample[i], past_len, cache_ckv[i],
                                              cache_krope[i], page_table, state_wkv[i], state_shift[i], w)
        outs_p.append(rest_p)
        outs_s.append(rest_s)
    stack = lambda outs, j: jnp.stack([o[j] for o in outs])
    return (xp, xs, stack(outs_p, 0), stack(outs_p, 1), stack(outs_p, 2), stack(outs_p, 3),
            stack(outs_s, 0), stack(outs_s, 1), stack(outs_s, 2), stack(outs_s, 3))
```

```python
import functools
import math

import jax
import jax.numpy as jnp
import numpy as np
from jax import lax
from jax.experimental import pallas as pl
from jax.experimental.pallas import tpu as pltpu

F32 = jnp.float32
BF16 = jnp.bfloat16

D_MODEL = 2048
MLA_HEADS = 8
QK_NOPE = 128
QK_ROPE = 64
QK_PAD = 256
V_HEAD = 128
KV_RANK = 512
ROPE_THETA = 10000.0
RW_HEADS = 16
RW_HEAD = 64
RW_WIDTH = RW_HEADS * RW_HEAD
DECAY_LORA = 64
AAA_LORA = 64
GATE_LORA = 160
RW_COLS = 3 * RW_WIDTH + DECAY_LORA + AAA_LORA + GATE_LORA
Q_COLS = MLA_HEADS * (QK_NOPE + QK_ROPE)
MLA_COLS = Q_COLS + KV_RANK + QK_ROPE
PEER_HEADS = 8
N_KEYS = 128
PK_DIM = 256
PK_HALF = PK_DIM // 2
PK_TOPK = 16
KEY_BITS = 7
TOPK_BITS = 4
PLE_DIM = 256
DEPTH = 1
ALPHA = (2 * DEPTH) ** 0.25
LN_EPS = 1e-5
RMS_EPS = 1e-6
GN_EPS = 64e-5
ATTN_SCALE = (QK_NOPE + QK_ROPE) ** -0.5

LANES = 128
SUBLANES = 8
RW_PAIRS = RW_HEADS // 2
DECODE_PAGES = 32
SCAN_DOT_PARTS = 4
SCAN_GROUP = SUBLANES
VMEM_LIMIT = 56 * 1024 * 1024

NT_DIMS = (((1,), (1,)), ((), ()))


def _cparams(sem, vmem=VMEM_LIMIT):
    return pltpu.CompilerParams(dimension_semantics=sem, vmem_limit_bytes=vmem)


def _resident(shape):
    nd = len(shape)
    return pl.BlockSpec(shape, lambda *_: (0,) * nd, pipeline_mode=pl.Buffered(1))


def _nt(a, b):
    return lax.dot_general(a, b, NT_DIMS, preferred_element_type=F32)


def _mm(a, b):
    return jnp.dot(a, b, preferred_element_type=F32)


def _dot01(x, e):
    h = x.astype(BF16)
    r = x - h.astype(F32)
    m = r.astype(BF16)
    l = (r - m.astype(F32)).astype(BF16)
    return _mm(h, e) + _mm(m, e) + _mm(l, e)


def _head_sum(x, e, et):
    return _dot01(_dot01(x, e), et)


def _sigmoid(x):
    return 1.0 / (1.0 + jnp.exp(-x))


def _layer_norm(x, g, b):
    mu = jnp.mean(x, axis=-1, keepdims=True)
    d = x - mu
    var = jnp.mean(d * d, axis=-1, keepdims=True)
    return d * lax.rsqrt(var + LN_EPS) * g + b


def _mla_proj_common(x_ref, w_ref, g_ref, cq_ref, sq_ref, ck_ref, sk_ref):
    x = x_ref[...].astype(BF16)
    res = _mm(x, w_ref[...])
    nq = MLA_HEADS * QK_NOPE
    rq = MLA_HEADS * QK_ROPE
    qn = res[:, :nq]
    qr = res[:, nq:nq + rq] * cq_ref[...] + res[:, nq + rq:nq + 2 * rq] * sq_ref[...]
    o = nq + 2 * rq
    c = res[:, o:o + KV_RANK]
    ckv = c * lax.rsqrt(jnp.mean(c * c, axis=-1, keepdims=True) + RMS_EPS) * g_ref[...]
    o += KV_RANK
    kr = res[:, o:o + QK_ROPE] * ck_ref[...] + res[:, o + QK_ROPE:o + 2 * QK_ROPE] * sk_ref[...]
    return qn, qr, ckv, kr


def _mla_proj_prompt_kernel(x_ref, w_ref, g_ref, cq_ref, sq_ref, ck_ref, sk_ref, wuk_ref, wuv_ref,
                            qc_o, kc_o, ckv_o, kr_o, v_o):
    qn, qr, ckv, kr = _mla_proj_common(x_ref, w_ref, g_ref, cq_ref, sq_ref, ck_ref, sk_ref)
    ckv_o[...] = ckv
    kr_o[...] = kr
    cb = ckv.astype(BF16)
    kn = _mm(cb, wuk_ref[...])
    v_o[...] = _mm(cb, wuv_ref[...]).astype(BF16)
    tm = qn.shape[0]
    low = lax.broadcasted_iota(jnp.int32, (tm, LANES), 1) < QK_ROPE
    kr_pad = jnp.concatenate([kr, jnp.zeros_like(kr)], axis=1).astype(BF16)
    for h in range(MLA_HEADS):
        blk = qr[:, (h // 2) * LANES:(h // 2 + 1) * LANES]
        if h % 2:
            blk = pltpu.roll(blk, QK_ROPE, axis=1)
        o = h * QK_PAD
        qc_o[:, o:o + QK_NOPE] = qn[:, h * QK_NOPE:(h + 1) * QK_NOPE].astype(BF16)
        qc_o[:, o + QK_NOPE:o + QK_PAD] = jnp.where(low, blk, 0.0).astype(BF16)
        kc_o[:, o:o + QK_NOPE] = kn[:, h * QK_NOPE:(h + 1) * QK_NOPE].astype(BF16)
        kc_o[:, o + QK_NOPE:o + QK_PAD] = kr_pad


def _mla_proj_sample_kernel(x_ref, w_ref, g_ref, cq_ref, sq_ref, ck_ref, sk_ref, wukt_ref,
                            ql_o, qr_o, ckv_o, kr_o):
    qn, qr, ckv, kr = _mla_proj_common(x_ref, w_ref, g_ref, cq_ref, sq_ref, ck_ref, sk_ref)
    qr_o[...] = qr.astype(BF16)
    ckv_o[...] = ckv
    kr_o[...] = kr
    for h in range(MLA_HEADS):
        qh = qn[:, h * QK_NOPE:(h + 1) * QK_NOPE].astype(BF16)
        ql_o[:, h * KV_RANK:(h + 1) * KV_RANK] = _mm(qh, wukt_ref[h]).astype(BF16)


def _mla_proj(x2d, w_all, kv_g, cq, sq, ck, sk, up_weights, prompt, tm):
    n = x2d.shape[0]
    wcols = w_all.shape[1]
    nq, rq = MLA_HEADS * QK_NOPE, MLA_HEADS * QK_ROPE
    row = lambda c: pl.BlockSpec((tm, c), lambda i: (i, 0))
    in_specs = [row(D_MODEL), _resident((D_MODEL, wcols)), _resident((1, KV_RANK)),
                row(rq), row(rq), row(QK_ROPE), row(QK_ROPE)]
    if prompt:
        kern = _mla_proj_prompt_kernel
        in_specs += [_resident((KV_RANK, nq)), _resident((KV_RANK, nq))]
        pad = MLA_HEADS * QK_PAD
        out_shape = [jax.ShapeDtypeStruct((n, pad), BF16), jax.ShapeDtypeStruct((n, pad), BF16),
                     jax.ShapeDtypeStruct((n, KV_RANK), F32), jax.ShapeDtypeStruct((n, QK_ROPE), F32),
                     jax.ShapeDtypeStruct((n, nq), BF16)]
        out_specs = [row(pad), row(pad), row(KV_RANK), row(QK_ROPE), row(nq)]
    else:
        kern = _mla_proj_sample_kernel
        in_specs += [_resident((MLA_HEADS, QK_NOPE, KV_RANK))]
        out_shape = [jax.ShapeDtypeStruct((n, MLA_HEADS * KV_RANK), BF16), jax.ShapeDtypeStruct((n, rq), BF16),
                     jax.ShapeDtypeStruct((n, KV_RANK), F32), jax.ShapeDtypeStruct((n, QK_ROPE), F32)]
        out_specs = [row(MLA_HEADS * KV_RANK), row(rq), row(KV_RANK), row(QK_ROPE)]
    return pl.pallas_call(
        kern, grid=(n // tm,), in_specs=in_specs, out_specs=out_specs, out_shape=out_shape,
        compiler_params=_cparams(("parallel",)), name="mla_proj",
    )(x2d, w_all, kv_g, cq, sq, ck, sk, *up_weights)


def _flash_kernel(q_ref, k_ref, v_ref, o_ref, m_ref, l_ref, acc_ref, s_ref, p_ref, *, tq, tk, nk):
    i = pl.program_id(1)
    j = pl.program_id(2)

    @pl.when(j == 0)
    def _():
        m_ref[...] = jnp.full(m_ref.shape, -jnp.inf, F32)
        l_ref[...] = jnp.zeros(l_ref.shape, F32)
        acc_ref[...] = jnp.zeros(acc_ref.shape, F32)

    def sweep(masked):
        if masked:
            qpos = i * tq + lax.broadcasted_iota(jnp.int32, (tq, tk), 0)
            kpos = j * tk + lax.broadcasted_iota(jnp.int32, (tq, tk), 1)
            visible = kpos <= qpos
        for h in range(MLA_HEADS):
            hs = slice(h * QK_PAD, (h + 1) * QK_PAD)
            s = _nt(q_ref[0, :, hs], k_ref[0, :, hs]) * ATTN_SCALE
            s_ref[h] = jnp.where(visible, s, -jnp.inf) if masked else s
        alphas = []
        for h in range(MLA_HEADS):
            s = s_ref[h]
            m_prev = m_ref[h]
            m_new = jnp.maximum(m_prev, jnp.max(s, axis=1, keepdims=True))
            alpha = jnp.exp(m_prev - m_new)
            p = jnp.exp(s - m_new)
            l_ref[h] = alpha * l_ref[h] + jnp.sum(p, axis=1, keepdims=True)
            m_ref[h] = m_new
            p_ref[h] = p.astype(BF16)
            alphas.append(alpha)
        for h in range(MLA_HEADS):
            acc_ref[h] = alphas[h] * acc_ref[h] + _mm(p_ref[h], v_ref[0, :, h * V_HEAD:(h + 1) * V_HEAD])

    first_q, last_q = i * tq, i * tq + tq - 1
    first_k, last_k = j * tk, j * tk + tk - 1

    @pl.when(last_k <= first_q)
    def _():
        sweep(False)

    @pl.when((last_k > first_q) & (first_k <= last_q))
    def _():
        sweep(True)

    @pl.when(j == nk - 1)
    def _():
        for h in range(MLA_HEADS):
            o_ref[0, :, h * V_HEAD:(h + 1) * V_HEAD] = (acc_ref[h] / l_ref[h]).astype(o_ref.dtype)


def _mla_prompt_attention(q, k, v, tq, tk):
    b, s, _ = q.shape
    nq, nk = s // tq, s // tk
    last = lambda i: (i * tq + tq - 1) // tk
    qspec = lambda c: pl.BlockSpec((1, tq, c), lambda bb, i, j: (bb, i, 0))
    kspec = lambda c: pl.BlockSpec((1, tk, c), lambda bb, i, j: (bb, jnp.minimum(j, last(i)), 0))
    return pl.pallas_call(
        functools.partial(_flash_kernel, tq=tq, tk=tk, nk=nk),
        grid=(b, nq, nk),
        in_specs=[qspec(q.shape[2]), kspec(k.shape[2]), kspec(v.shape[2])],
        out_specs=qspec(v.shape[2]),
        out_shape=jax.ShapeDtypeStruct(v.shape, BF16),
        scratch_shapes=[pltpu.VMEM((MLA_HEADS, tq, 1), F32), pltpu.VMEM((MLA_HEADS, tq, 1), F32),
                        pltpu.VMEM((MLA_HEADS, tq, V_HEAD), F32),
                        pltpu.VMEM((MLA_HEADS, tq, tk), F32), pltpu.VMEM((MLA_HEADS, tq, tk), BF16)],
        compiler_params=_cparams(("parallel", "parallel", "arbitrary")), name="mla_prompt_attention",
    )(q, k, v)


def _decode_kernel(pt_ref, ql_ref, qr_ref, cn_ref, kn_ref, ckv_hbm, krt_hbm, o_ref,
                   m_ref, l_ref, acc_ref, cbuf, kbuf, csem, ksem, *, pages, ng, n_steps):
    step = pl.program_id(0) * ng + pl.program_id(1)
    slot = step % 2

    def copies(s, sl):
        out = []
        for j in range(pages):
            pg = pt_ref[s * pages + j]
            out.append(pltpu.make_async_copy(ckv_hbm.at[pg], cbuf.at[sl, j], csem.at[sl]))
            out.append(pltpu.make_async_copy(krt_hbm.at[pg], kbuf.at[sl, j], ksem.at[sl]))
        return out

    @pl.when(step == 0)
    def _():
        for cp in copies(step, slot):
            cp.start()

    @pl.when(step + 1 < n_steps)
    def _():
        for cp in copies(step + 1, 1 - slot):
            cp.start()

    for cp in copies(step, slot):
        cp.wait()
    _decode_init(m_ref, l_ref, acc_ref)
    _decode_sweep(ql_ref, qr_ref, [cbuf.at[slot, j] for j in range(pages)],
                  [kbuf.at[slot, j] for j in range(pages)], m_ref, l_ref, acc_ref)
    _decode_final(ql_ref, qr_ref, cn_ref, kn_ref, o_ref, m_ref, l_ref, acc_ref, ng)


def _decode_init(m_ref, l_ref, acc_ref):
    @pl.when(pl.program_id(1) == 0)
    def _():
        m_ref[...] = jnp.full(m_ref.shape, -jnp.inf, F32)
        l_ref[...] = jnp.zeros(l_ref.shape, F32)
        acc_ref[...] = jnp.zeros(acc_ref.shape, F32)


def _decode_sweep(ql_ref, qr_ref, c_refs, k_refs, m_ref, l_ref, acc_ref):
    pages = len(c_refs)
    ql = ql_ref[0]
    qr = qr_ref[0]
    cs, ss = [], []
    for i in range(pages):
        c = c_refs[i][...].astype(BF16)
        k = k_refs[i][...].astype(BF16)
        cs.append(c)
        ss.append(_nt(ql, c) + _mm(qr, k))
    s = jnp.concatenate(ss, axis=1) * ATTN_SCALE
    m_prev = m_ref[...]
    m_new = jnp.maximum(m_prev, jnp.max(s, axis=1, keepdims=True))
    alpha = jnp.exp(m_prev - m_new)
    p = jnp.exp(s - m_new)
    pb = p.astype(BF16)
    page = cs[0].shape[0]
    pv = _mm(pb[:, :page], cs[0])
    for i in range(1, pages):
        pv = pv + _mm(pb[:, i * page:(i + 1) * page], cs[i])
    m_ref[...] = m_new
    l_ref[...] = alpha * l_ref[...] + jnp.sum(p, axis=1, keepdims=True)
    acc_ref[...] = alpha * acc_ref[...] + pv


def _decode_final(ql_ref, qr_ref, cn_ref, kn_ref, o_ref, m_ref, l_ref, acc_ref, ng):
    @pl.when(pl.program_id(1) == ng - 1)
    def _():
        ql, qr = ql_ref[0], qr_ref[0]
        m_new = m_ref[...]
        cn = cn_ref[0].astype(BF16).astype(F32)
        kn = kn_ref[0].astype(BF16).astype(F32)
        s_self = (jnp.sum(ql.astype(F32) * cn, axis=1, keepdims=True)
                  + jnp.sum(qr.astype(F32) * kn, axis=1, keepdims=True)) * ATTN_SCALE
        m_f = jnp.maximum(m_new, s_self)
        a_f = jnp.exp(m_new - m_f)
        p_self = jnp.exp(s_self - m_f)
        l_f = a_f * l_ref[...] + p_self
        acc_f = a_f * acc_ref[...] + p_self.astype(BF16).astype(F32) * cn
        o_ref[0] = acc_f / l_f


def _mla_sample_attention(page_table, q_lat, q_rope, c_new, kr_new, cache_ckv, cache_krope_t, pages):
    db, n_pages = page_table.shape
    _, page, _ = cache_ckv.shape
    ng = n_pages // pages
    pt = page_table.reshape(-1)
    per_seq = lambda shape: pl.BlockSpec((1,) + shape, lambda b, g, pt_: (b, 0, 0))
    in_hbm = pl.BlockSpec(memory_space=pl.ANY)
    grid_spec = pltpu.PrefetchScalarGridSpec(
        num_scalar_prefetch=1, grid=(db, ng),
        in_specs=[per_seq((MLA_HEADS, KV_RANK)), per_seq((MLA_HEADS, QK_ROPE)),
                  per_seq((1, KV_RANK)), per_seq((1, QK_ROPE)), in_hbm, in_hbm],
        out_specs=per_seq((MLA_HEADS, KV_RANK)),
        scratch_shapes=[pltpu.VMEM((MLA_HEADS, 1), F32), pltpu.VMEM((MLA_HEADS, 1), F32),
                        pltpu.VMEM((MLA_HEADS, KV_RANK), F32),
                        pltpu.VMEM((2, pages, page, KV_RANK), F32), pltpu.VMEM((2, pages, QK_ROPE, page), F32),
                        pltpu.SemaphoreType.DMA((2,)), pltpu.SemaphoreType.DMA((2,))])
    return pl.pallas_call(
        functools.partial(_decode_kernel, pages=pages, ng=ng, n_steps=db * ng),
        grid_spec=grid_spec,
        out_shape=jax.ShapeDtypeStruct((db, MLA_HEADS, KV_RANK), F32),
        compiler_params=_cparams(("arbitrary", "arbitrary")), name="mla_sample_attention",
    )(pt, q_lat, q_rope, c_new, kr_new, cache_ckv, cache_krope_t)


def _head_up_kernel(o_ref, w_ref, out_ref):
    out_ref[...] = _mm(o_ref[...].astype(BF16), w_ref[...]).astype(out_ref.dtype)


def _head_up(o_lat2d, w_uv2):
    n = o_lat2d.shape[0]
    return pl.pallas_call(
        _head_up_kernel, grid=(MLA_HEADS,),
        in_specs=[pl.BlockSpec((n, KV_RANK), lambda h: (0, h)), pl.BlockSpec((KV_RANK, V_HEAD), lambda h: (0, h))],
        out_specs=pl.BlockSpec((n, V_HEAD), lambda h: (0, h)),
        out_shape=jax.ShapeDtypeStruct((n, MLA_HEADS * V_HEAD), BF16),
        compiler_params=_cparams(("parallel",)), name="mla_head_up",
    )(o_lat2d, w_uv2)


def _rwkv_terms(p, prev, mu_ref, w0_ref, wup_ref, a0_ref, aup_ref, gup_ref, kk_ref, ka_ref, rk_ref,
                e_ref, et_ref, outs):
    r_o, w_o, k_o, v_o, n_o, b_o, g_o, bonus_o = outs
    xs = p + mu_ref[...] * (prev - p)
    w3 = 3 * RW_WIDTH
    r = xs[:, :RW_WIDTH]
    k0 = xs[:, RW_WIDTH:2 * RW_WIDTH]
    v = xs[:, 2 * RW_WIDTH:w3]
    xw = xs[:, w3:w3 + DECAY_LORA]
    xa = xs[:, w3 + DECAY_LORA:w3 + DECAY_LORA + AAA_LORA]
    xg = xs[:, w3 + DECAY_LORA + AAA_LORA:]
    y = w0_ref[...] + _mm(jnp.tanh(xw).astype(BF16), wup_ref[...])
    decay = jnp.exp(-math.exp(-0.5) * _sigmoid(y))
    a = _sigmoid(a0_ref[...] + _mm(xa.astype(BF16), aup_ref[...]))
    g = _mm(_sigmoid(xg).astype(BF16), gup_ref[...])
    kk = k0 * kk_ref[...]
    e, et = e_ref[...], et_ref[...]
    kk = kk * lax.rsqrt(jnp.maximum(_head_sum(kk * kk, e, et), 1e-24))
    k = k0 * (1.0 + (a - 1.0) * ka_ref[...])
    r_o[...] = r
    w_o[...] = decay
    k_o[...] = k
    v_o[...] = v
    n_o[...] = -kk
    b_o[...] = kk * a
    g_o[...] = g
    bonus_o[...] = _head_sum(r * k * rk_ref[...], e, et) * v


def _rwkv_proj_prompt_kernel(x_ref, w_ref, mu_ref, w0_ref, wup_ref, a0_ref, aup_ref, gup_ref,
                             kk_ref, ka_ref, rk_ref, e_ref, et_ref,
                             r_o, w_o, k_o, v_o, n_o, b_o, g_o, bonus_o, last_o, carry_ref):
    tm = x_ref.shape[1]

    @pl.when(pl.program_id(1) == 0)
    def _():
        carry_ref[...] = jnp.zeros(carry_ref.shape, F32)

    p = _mm(x_ref[0].astype(BF16), w_ref[...])
    rows = lax.broadcasted_iota(jnp.int32, p.shape, 0)
    prev = jnp.where(rows == 0, carry_ref[...], pltpu.roll(p, 1, axis=0))
    last = p[tm - 1:tm, :]
    carry_ref[...] = last
    last_o[0] = last
    outs = tuple(o.at[0] for o in (r_o, w_o, k_o, v_o, n_o, b_o, g_o, bonus_o))
    _rwkv_terms(p, prev, mu_ref, w0_ref, wup_ref, a0_ref, aup_ref, gup_ref, kk_ref, ka_ref, rk_ref,
                e_ref, et_ref, outs)


def _rwkv_proj_sample_kernel(x_ref, prev_ref, w_ref, mu_ref, w0_ref, wup_ref, a0_ref, aup_ref, gup_ref,
                             kk_ref, ka_ref, rk_ref, e_ref, et_ref,
                             r_o, w_o, k_o, v_o, n_o, b_o, g_o, bonus_o, last_o):
    p = _mm(x_ref[...].astype(BF16), w_ref[...])
    last_o[...] = p
    _rwkv_terms(p, prev_ref[...], mu_ref, w0_ref, wup_ref, a0_ref, aup_ref, gup_ref, kk_ref, ka_ref, rk_ref,
                e_ref, et_ref, (r_o, w_o, k_o, v_o, n_o, b_o, g_o, bonus_o))


def _rwkv_weight_specs():
    vec = lambda c: _resident((1, c))
    return [_resident((D_MODEL, RW_COLS)), vec(RW_COLS), vec(RW_WIDTH), _resident((DECAY_LORA, RW_WIDTH)),
            vec(RW_WIDTH), _resident((AAA_LORA, RW_WIDTH)), _resident((GATE_LORA, RW_WIDTH)),
            vec(RW_WIDTH), vec(RW_WIDTH), vec(RW_WIDTH),
            _resident((RW_WIDTH, LANES)), _resident((LANES, RW_WIDTH))]


def _rwkv_proj_prompt(x, rw_weights, tm):
    b, t, _ = x.shape
    tile = lambda c: pl.BlockSpec((1, tm, c), lambda bb, i: (bb, i, 0))
    wide = jax.ShapeDtypeStruct((b, t, RW_WIDTH), F32)
    return pl.pallas_call(
        _rwkv_proj_prompt_kernel, grid=(b, t // tm),
        in_specs=[tile(D_MODEL)] + _rwkv_weight_specs(),
        out_specs=[tile(RW_WIDTH)] * 8 + [pl.BlockSpec((1, 1, RW_COLS), lambda bb, i: (bb, 0, 0))],
        out_shape=[wide] * 8 + [jax.ShapeDtypeStruct((b, 1, RW_COLS), F32)],
        scratch_shapes=[pltpu.VMEM((1, RW_COLS), F32)],
        compiler_params=_cparams(("parallel", "arbitrary")), name="rwkv_proj_prompt",
    )(x, *rw_weights)


def _rwkv_proj_sample(x2d, prev, rw_weights):
    n = x2d.shape[0]
    full = lambda c: pl.BlockSpec((n, c), lambda i: (0, 0))
    wide = jax.ShapeDtypeStruct((n, RW_WIDTH), F32)
    return pl.pallas_call(
        _rwkv_proj_sample_kernel, grid=(1,),
        in_specs=[full(D_MODEL), full(RW_COLS)] + _rwkv_weight_specs(),
        out_specs=[full(RW_WIDTH)] * 8 + [full(RW_COLS)],
        out_shape=[wide] * 8 + [jax.ShapeDtypeStruct((n, RW_COLS), F32)],
        compiler_params=_cparams(("arbitrary",)), name="rwkv_proj_sample",
    )(x2d, prev, *rw_weights)


def _scan_load_state(s0_ref, state_ref):
    for b in range(state_ref.shape[0]):
        for p in range(RW_PAIRS):
            state_ref[b, p] = jnp.concatenate([s0_ref[b, 2 * p], s0_ref[b, 2 * p + 1]], axis=1)


def _scan_store_state(st_ref, state_ref):
    for b in range(state_ref.shape[0]):
        for p in range(RW_PAIRS):
            s = state_ref[b, p]
            st_ref[b, 2 * p] = s[:, :RW_HEAD]
            st_ref[b, 2 * p + 1] = s[:, RW_HEAD:]


def _scan_steps(ins, e2_ref, eo_ref, vt_ref, ev_ref, out_ref, state_ref, vcol_ref, group):
    nb = state_ref.shape[0]
    chains = [(b, p) for b in range(nb) for p in range(RW_PAIRS)]
    lane = lax.broadcasted_iota(jnp.int32, (RW_HEAD, LANES), 1)
    sub = lax.broadcasted_iota(jnp.int32, (RW_HEAD, LANES), 0)
    diag = (lane & (RW_HEAD - 1)) == sub
    e2 = e2_ref[...]

    def pieces(x):
        hi = x.astype(BF16)
        return jnp.concatenate([hi, (x - hi.astype(F32)).astype(BF16)], axis=1)

    def head_sums(xs):
        out = []
        per = max(len(xs) // SCAN_DOT_PARTS, 1)
        for q in range(0, len(xs), per):
            res = _mm(jnp.concatenate([pieces(x) for x in xs[q:q + per]], axis=0), e2)
            out += [res[i * RW_HEAD:(i + 1) * RW_HEAD] for i in range(per)]
        return out

    def head_sums_out(xs):
        eo = eo_ref[...]
        out = []
        per = max(len(xs) // SCAN_DOT_PARTS, 2)
        for q in range(0, len(xs), per):
            res = _mm(jnp.concatenate([jnp.concatenate([xs[i].astype(BF16), xs[i + 1].astype(BF16)], axis=1)
                                       for i in range(q, q + per, 2)], axis=0), eo)
            for i in range(per // 2):
                blk = res[i * RW_HEAD:(i + 1) * RW_HEAD]
                out += [blk[:, :LANES], blk[:, LANES:]]
        return out

    def steps(tg):
        t0 = tg * group if isinstance(tg, int) else pl.multiple_of(tg * group, group)
        rows = [[ref[b, pl.ds(t0, group), p * LANES:(p + 1) * LANES] for ref in ins] for b, p in chains]
        outs = [[] for _ in chains]
        if group > 1:
            lhs = [pieces(jnp.where((lane % RW_HEADS) // 2 == p, vt_ref[b, tg], 0.0)) for b, p in chains]
            vcol_ref[...] = _mm(jnp.concatenate(lhs, axis=0), ev_ref[...])
        for i in range(group):
            r, w, k, v, n, bb = ([x[j][i:i + 1, :] for x in rows] for j in range(len(ins)))
            if group > 1:
                v_cols = [vcol_ref[ci * RW_HEAD:(ci + 1) * RW_HEAD, i * LANES:(i + 1) * LANES]
                          for ci in range(len(chains))]
            else:
                v_cols = head_sums([jnp.where(diag, vi, 0.0) for vi in v])
            sa = head_sums([state_ref[b, p] * n[ci] for ci, (b, p) in enumerate(chains)])
            new = []
            for ci, (b, p) in enumerate(chains):
                s = state_ref[b, p] * w[ci] + sa[ci] * bb[ci] + v_cols[ci] * k[ci]
                state_ref[b, p] = s
                new.append(s * r[ci])
            for ci, o_col in enumerate(head_sums_out(new)):
                outs[ci].append(jnp.sum(jnp.where(diag, o_col, 0.0), axis=0, keepdims=True))
        for ci, (b, p) in enumerate(chains):
            out_ref[b, pl.ds(t0, group), p * LANES:(p + 1) * LANES] = (
                outs[ci][0] if group == 1 else jnp.concatenate(outs[ci], axis=0))

    return steps


def _scan_loop(steps, n_groups):
    if n_groups == 1:
        steps(0)
    else:
        def body(tg, carry):
            steps(tg)
            return carry
        lax.fori_loop(0, n_groups, body, 0)


def _rwkv_scan_kernel(s0_ref, r_ref, w_ref, k_ref, v_ref, n_ref, b_ref, e2_ref, eo_ref, *rest, tc, nc):
    c = pl.program_id(1)
    group = SCAN_GROUP if tc % SCAN_GROUP == 0 else 1
    if group > 1:
        vt_ref, ev_ref, out_ref, st_ref, state_ref, vcol_ref = rest
    else:
        (out_ref, st_ref, state_ref), vt_ref, ev_ref, vcol_ref = rest, None, None, None

    @pl.when(c == 0)
    def _():
        _scan_load_state(s0_ref, state_ref)

    _scan_loop(_scan_steps((r_ref, w_ref, k_ref, v_ref, n_ref, b_ref), e2_ref, eo_ref, vt_ref, ev_ref,
                           out_ref, state_ref, vcol_ref, group), tc // group)

    @pl.when(c == nc - 1)
    def _():
        _scan_store_state(st_ref, state_ref)


def _scan_constants(e2):
    same_head = e2[:LANES]
    zero = jnp.zeros_like(same_head)
    eo = jnp.concatenate([jnp.concatenate([same_head, zero], axis=1),
                          jnp.concatenate([zero, same_head], axis=1)], axis=0)
    src = jnp.arange(2 * LANES) % LANES
    dst = jnp.arange(SCAN_GROUP * LANES)
    ev = ((src[:, None] // RW_HEADS == dst[None, :] // LANES)
          & (src[:, None] % 2 == (dst[None, :] % LANES) // RW_HEAD))
    return eo, ev.astype(BF16)


def _transposed_v(v):
    b, t, _ = v.shape
    return jnp.swapaxes(v.reshape(b, t // SCAN_GROUP, SCAN_GROUP * RW_HEADS, RW_HEAD), 2, 3)


def _rwkv_scan(s0, r, w, k, v, n, bb, e2, nb, tc):
    b, t, _ = r.shape
    nc = t // tc
    seq = pl.BlockSpec((nb, tc, RW_WIDTH), lambda i, c: (i, c, 0))
    st = pl.BlockSpec((nb, RW_HEADS, RW_HEAD, RW_HEAD), lambda i, c: (i, 0, 0, 0))
    eo, ev = _scan_constants(e2)
    in_specs = [st] + [seq] * 6 + [_resident(e2.shape), _resident(eo.shape)]
    args = [s0, r, w, k, v, n, bb, e2, eo]
    scratch = [pltpu.VMEM((nb, RW_PAIRS, RW_HEAD, LANES), F32)]
    if tc % SCAN_GROUP == 0:
        in_specs += [pl.BlockSpec((nb, tc // SCAN_GROUP, RW_HEAD, LANES), lambda i, c: (i, c, 0, 0)),
                     _resident(ev.shape)]
        args += [_transposed_v(v), ev]
        scratch.append(pltpu.VMEM((nb * RW_PAIRS * RW_HEAD, SCAN_GROUP * LANES), F32))
    return pl.pallas_call(
        functools.partial(_rwkv_scan_kernel, tc=tc, nc=nc),
        grid=(b // nb, nc),
        in_specs=in_specs,
        out_specs=[seq, st],
        out_shape=[jax.ShapeDtypeStruct((b, t, RW_WIDTH), F32),
                   jax.ShapeDtypeStruct((b, RW_HEADS, RW_HEAD, RW_HEAD), F32)],
        scratch_shapes=scratch,
        compiler_params=_cparams(("parallel", "arbitrary")), name="rwkv_scan",
    )(*args)


def _mix_out_kernel(x_ref, om_ref, raw_ref, g_ref, bonus_ref, gng_ref, gnb_ref, e_ref, et_ref,
                    wo1_ref, wo2_ref, lng_ref, lnb_ref, o_ref):
    e, et = e_ref[...], et_ref[...]
    raw = raw_ref[...]
    mu = _head_sum(raw, e, et) * (1.0 / RW_HEAD)
    d = raw - mu
    var = _head_sum(d * d, e, et) * (1.0 / RW_HEAD)
    o_rw = (d * lax.rsqrt(var + GN_EPS) * gng_ref[...] + gnb_ref[...] + bonus_ref[...]) * g_ref[...]
    mixed = _mm(om_ref[...], wo1_ref[...]) + _mm(o_rw.astype(BF16), wo2_ref[...])
    o_ref[...] = _layer_norm(ALPHA * x_ref[...] + mixed, lng_ref[...], lnb_ref[...])


def _mix_out(x2d, o_mla, raw, g, bonus, gng, gnb, e, et, wo1, wo2, lng, lnb, tm):
    n = x2d.shape[0]
    row = lambda c: pl.BlockSpec((tm, c), lambda i: (i, 0))
    vec = lambda c: _resident((1, c))
    return pl.pallas_call(
        _mix_out_kernel, grid=(n // tm,),
        in_specs=[row(D_MODEL), row(RW_WIDTH), row(RW_WIDTH), row(RW_WIDTH), row(RW_WIDTH),
                  vec(RW_WIDTH), vec(RW_WIDTH), _resident((RW_WIDTH, LANES)), _resident((LANES, RW_WIDTH)),
                  _resident(wo1.shape), _resident(wo2.shape), vec(D_MODEL), vec(D_MODEL)],
        out_specs=row(D_MODEL),
        out_shape=jax.ShapeDtypeStruct((n, D_MODEL), F32),
        compiler_params=_cparams(("parallel",)), name="mix_out_ln1",
    )(x2d, o_mla, raw, g, bonus, gng, gnb, e, et, wo1, wo2, lng, lnb)


def _peer_route_kernel(x_ref, wq_ref, k1_ref, k2_ref, pa_ref, pb_ref, flat_ref, i1_o, i2_o, g_o):
    tm = x_ref.shape[0]
    kk2 = PK_TOPK * PK_TOPK
    q = _mm(x_ref[...].astype(BF16), wq_ref[...])
    key_row = lax.broadcasted_iota(jnp.int32, (N_KEYS, tm), 0).astype(F32)
    rank_row = lax.broadcasted_iota(jnp.int32, (PK_TOPK, tm), 0)
    neg = -jnp.inf
    pa, pb = pa_ref[...], pb_ref[...]
    flat = jnp.concatenate([flat_ref[...]] * -(-tm // LANES), axis=1)[:, :tm]
    is_pair = flat < float(kk2)

    def pick(p, x):
        h = x.astype(BF16)
        r = x - h.astype(F32)
        m = r.astype(BF16)
        l = (r - m.astype(F32)).astype(BF16)
        return _mm(p, h) + _mm(p, m) + _mm(p, l)

    def top_keys(scores):
        def body(r, carry):
            hit = rank_row == r
            ms = [jnp.max(s, axis=0, keepdims=True) for s, _, _ in carry]
            ids = [jnp.min(jnp.where(s == m, key_row, float(N_KEYS)), axis=0, keepdims=True)
                   for (s, _, _), m in zip(carry, ms)]
            return tuple((jnp.where(key_row == idx, neg, s), jnp.where(hit, m, vals), jnp.where(hit, idx, idxs))
                         for (s, vals, idxs), m, idx in zip(carry, ms, ids))
        zero = jnp.zeros((PK_TOPK, tm), F32)
        res = lax.fori_loop(0, PK_TOPK, body, tuple((s, zero, zero) for s in scores))
        return [(vals, idxs) for _, vals, idxs in res]

    experts, gates = [], []
    for h in range(PEER_HEADS):
        q1 = q[:, h * PK_DIM:h * PK_DIM + PK_HALF].astype(BF16)
        q2 = q[:, h * PK_DIM + PK_HALF:(h + 1) * PK_DIM].astype(BF16)
        (v1, i1), (v2, i2) = top_keys([_nt(k1_ref[h], q1), _nt(k2_ref[h], q2)])
        cand = jnp.where(is_pair, pick(pa, v1) + pick(pb, v2), neg)
        expert = pick(pa, i1) * float(N_KEYS) + pick(pb, i2)

        def body(r, carry, expert=expert):
            cnd, top, den, e_a, p_a = carry
            m = jnp.max(cnd, axis=0, keepdims=True)
            idx = jnp.min(jnp.where(cnd == m, flat, float(kk2)), axis=0, keepdims=True)
            hit = flat == idx
            e = jnp.max(jnp.where(hit, expert, -1.0), axis=0, keepdims=True)
            cnd = jnp.where(hit, neg, cnd)
            top = jnp.where(r == 0, m, top)
            pr = jnp.exp(m - top)
            sel = rank_row == r
            return cnd, top, den + pr, jnp.where(sel, e, e_a), jnp.where(sel, pr, p_a)

        zrow = jnp.zeros((1, tm), F32)
        zk = jnp.zeros((PK_TOPK, tm), F32)
        _, _, den, e_h, p_h = lax.fori_loop(0, PK_TOPK, body, (cand, zrow, zrow, zk, zk))
        experts.append(e_h)
        gates.append(p_h / den)
    e_int = jnp.concatenate(experts, axis=0).T.astype(jnp.int32)
    i1_o[...] = e_int >> KEY_BITS
    i2_o[...] = e_int & (N_KEYS - 1)
    g_o[...] = jnp.concatenate(gates, axis=0).T


def _candidate_pairs():
    pairs = [(a, b) for a in range(PK_TOPK) for b in range(PK_TOPK // (a + 1))]
    rows = -(-len(pairs) // 16) * 16
    pa = np.zeros((rows, PK_TOPK), np.float32)
    pb = np.zeros((rows, PK_TOPK), np.float32)
    flat = np.full((rows, LANES), float(PK_TOPK * PK_TOPK), np.float32)
    for i, (a, b) in enumerate(pairs):
        pa[i, a] = 1.0
        pb[i, b] = 1.0
        flat[i, :] = a * PK_TOPK + b
    return jnp.asarray(pa, BF16), jnp.asarray(pb, BF16), jnp.asarray(flat)


def _peer_route(x2d, wq, k1, k2, tm):
    n = x2d.shape[0]
    slots = PEER_HEADS * PK_TOPK
    row = lambda c: pl.BlockSpec((tm, c), lambda i: (i, 0))
    pa, pb, flat = _candidate_pairs()
    return pl.pallas_call(
        _peer_route_kernel, grid=(n // tm,),
        in_specs=[row(D_MODEL), _resident(wq.shape), _resident(k1.shape), _resident(k2.shape),
                  _resident(pa.shape), _resident(pb.shape), _resident(flat.shape)],
        out_specs=[row(slots)] * 3,
        out_shape=[jax.ShapeDtypeStruct((n, slots), jnp.int32), jax.ShapeDtypeStruct((n, slots), jnp.int32),
                   jax.ShapeDtypeStruct((n, slots), F32)],
        compiler_params=_cparams(("parallel",)), name="peer_route",
    )(x2d, wq, k1, k2, pa, pb, flat)


def _peer_weights_kernel(i1_ref, i2_ref, g_ref, wt_ref):
    tm = i1_ref.shape[0]
    key = lax.broadcasted_iota(jnp.int32, (N_KEYS, PEER_HEADS * PK_TOPK), 0)

    def body(grp, carry):
        t0 = pl.multiple_of(grp * SUBLANES, SUBLANES)
        i1s = i1_ref[pl.ds(t0, SUBLANES), :]
        i2s = i2_ref[pl.ds(t0, SUBLANES), :]
        gs = g_ref[pl.ds(t0, SUBLANES), :]
        per_token = []
        for j in range(SUBLANES):
            g1 = jnp.where(key == i1s[j:j + 1, :], gs[j:j + 1, :], 0.0)
            hi = g1.astype(BF16)
            lo = (g1 - hi.astype(F32)).astype(BF16)
            g2 = jnp.where(key == i2s[j:j + 1, :], 1.0, 0.0).astype(BF16)
            per_token.append(_nt(jnp.concatenate([hi, lo], axis=1), jnp.concatenate([g2, g2], axis=1)))
        wt_ref[grp] = jnp.swapaxes(jnp.stack(per_token, axis=0), 0, 1)
        return carry

    lax.fori_loop(0, tm // SUBLANES, body, 0)


def _peer_weights(i1, i2, g, tm):
    n, slots = i1.shape
    row = pl.BlockSpec((tm, slots), lambda i: (i, 0))
    return pl.pallas_call(
        _peer_weights_kernel, grid=(n // tm,),
        in_specs=[row, row, row],
        out_specs=pl.BlockSpec((tm // SUBLANES, N_KEYS, SUBLANES, N_KEYS), lambda i: (i, 0, 0, 0)),
        out_shape=jax.ShapeDtypeStruct((n // SUBLANES, N_KEYS, SUBLANES, N_KEYS), F32),
        compiler_params=_cparams(("parallel",)), name="peer_weights",
    )(i1, i2, g)


def _peer_dense_kernel(x_ref, wt_ref, u_ref, v_ref, y_ref, xb_ref, z_ref, *, rows_per_step):
    c = pl.program_id(1)
    tt = x_ref.shape[0]

    @pl.when(c == 0)
    def _():
        xb_ref[...] = x_ref[...].astype(BF16)
        y_ref[...] = jnp.zeros(y_ref.shape, F32)

    hid = _nt(xb_ref[...], u_ref[...])
    for a in range(rows_per_step):
        h = hid[:, a * N_KEYS:(a + 1) * N_KEYS]
        w = wt_ref[:, a].reshape(tt, N_KEYS)
        gelu = 0.5 * h * (1.0 + lax.erf(h * math.sqrt(0.5)))
        z_ref[:, a * N_KEYS:(a + 1) * N_KEYS] = (w * gelu).astype(BF16)
    y_ref[...] += _mm(z_ref[...], v_ref[...])


def _peer_dense(x2d, wt, u, v, tt, rows_per_step):
    n = x2d.shape[0]
    ne = rows_per_step * N_KEYS
    return pl.pallas_call(
        functools.partial(_peer_dense_kernel, rows_per_step=rows_per_step),
        grid=(n // tt, N_KEYS // rows_per_step),
        in_specs=[pl.BlockSpec((tt, D_MODEL), lambda i, c: (i, 0)),
                  pl.BlockSpec((tt // SUBLANES, rows_per_step, SUBLANES, N_KEYS), lambda i, c: (i, c, 0, 0)),
                  pl.BlockSpec((ne, D_MODEL), lambda i, c: (c, 0)),
                  pl.BlockSpec((ne, D_MODEL), lambda i, c: (c, 0))],
        out_specs=pl.BlockSpec((tt, D_MODEL), lambda i, c: (i, 0)),
        out_shape=jax.ShapeDtypeStruct((n, D_MODEL), F32),
        scratch_shapes=[pltpu.VMEM((tt, D_MODEL), BF16), pltpu.VMEM((tt, ne), BF16)],
        compiler_params=_cparams(("parallel", "arbitrary")), name="peer_dense",
    )(x2d, wt, u, v)


def _ple_kernel(x1_ref, y_ref, p_ref, l2g_ref, l2b_ref, wg_ref, wp_ref, l3g_ref, l3b_ref, o_ref):
    x2 = _layer_norm(ALPHA * x1_ref[...] + y_ref[...], l2g_ref[...], l2b_ref[...])
    gate = _sigmoid(_mm(x2.astype(BF16), wg_ref[...]))
    ple = _mm(p_ref[...].astype(BF16), wp_ref[...]) * gate
    o_ref[...] = _layer_norm(ALPHA * x2 + ple, l3g_ref[...], l3b_ref[...])


def _ple(x1, y, p_emb, l2g, l2b, wg, wp, l3g, l3b, tm):
    n = x1.shape[0]
    row = lambda c: pl.BlockSpec((tm, c), lambda i: (i, 0))
    vec = _resident((1, D_MODEL))
    return pl.pallas_call(
        _ple_kernel, grid=(n // tm,),
        in_specs=[row(D_MODEL), row(D_MODEL), row(PLE_DIM), vec, vec, _resident(wg.shape), _resident(wp.shape), vec, vec],
        out_specs=row(D_MODEL),
        out_shape=jax.ShapeDtypeStruct((n, D_MODEL), F32),
        compiler_params=_cparams(("parallel",)), name="ple_ln2_ln3",
    )(x1, y, p_emb, l2g, l2b, wg, wp, l3g, l3b)


def _rope_tables(pos):
    inv = ROPE_THETA ** (-jnp.arange(0, QK_ROPE, 2, dtype=F32) / QK_ROPE)
    ang = pos.astype(F32)[:, None] * inv[None, :]
    cos, sin = jnp.cos(ang), jnp.sin(ang)
    ck = jnp.concatenate([cos, cos], axis=-1)
    sk = jnp.concatenate([-sin, sin], axis=-1)
    return jnp.tile(ck, (1, MLA_HEADS)), jnp.tile(sk, (1, MLA_HEADS)), ck, sk


def _prepare_weights(w_in, kv_norm_g, w_uk, w_uv, rw_mu, rw_w0, rw_w_up, rw_a0, rw_a_up, rw_g_up,
                     rw_k_k, rw_k_a, rw_r_k, rw_gn_g, rw_gn_b, w_o, ln1_g, ln1_b,
                     peer_wq, peer_k1, peer_k2, peer_u, peer_v, ln2_g, ln2_b,
                     ple_w, ple_gate_w, ln3_g, ln3_b):
    half = QK_ROPE // 2
    swap = lambda w: jnp.concatenate([w[..., half:], w[..., :half]], axis=-1)
    wq = w_in[:, :Q_COLS].reshape(D_MODEL, MLA_HEADS, QK_NOPE + QK_ROPE)
    wq_n = wq[:, :, :QK_NOPE].reshape(D_MODEL, -1)
    wq_r = wq[:, :, QK_NOPE:]
    w_c = w_in[:, Q_COLS:Q_COLS + KV_RANK]
    w_kr = w_in[:, Q_COLS + KV_RANK:MLA_COLS]
    w_mla = jnp.concatenate([wq_n, wq_r.reshape(D_MODEL, -1), swap(wq_r).reshape(D_MODEL, -1),
                             w_c, w_kr, swap(w_kr)], axis=1).astype(BF16)
    row = lambda a: a.reshape(1, -1)
    head_of_lane = jnp.arange(RW_WIDTH) // RW_HEAD
    e = (head_of_lane[:, None] == jnp.arange(LANES)[None, :]).astype(BF16)
    half_of = (jnp.arange(2 * LANES) % LANES) // RW_HEAD
    e2 = (half_of[:, None] == half_of[None, :LANES]).astype(BF16)
    return dict(
        e2=e2,
        w_mla=w_mla, kv_g=row(kv_norm_g),
        w_uk2=w_uk.reshape(KV_RANK, -1).astype(BF16), w_uv2=w_uv.reshape(KV_RANK, -1).astype(BF16),
        w_ukt=jnp.transpose(w_uk, (1, 2, 0)).astype(BF16),
        rw=(w_in[:, MLA_COLS:].astype(BF16), row(rw_mu), row(rw_w0), rw_w_up.astype(BF16), row(rw_a0),
            rw_a_up.astype(BF16), rw_g_up.astype(BF16), row(rw_k_k), row(rw_k_a), row(rw_r_k), e, e.T),
        gn_g=row(rw_gn_g), gn_b=row(rw_gn_b), e=e, et=e.T,
        wo1=w_o[:MLA_HEADS * V_HEAD].astype(BF16), wo2=w_o[MLA_HEADS * V_HEAD:].astype(BF16),
        ln1_g=row(ln1_g), ln1_b=row(ln1_b),
        peer_wq=peer_wq.astype(BF16), peer_k1=peer_k1.astype(BF16), peer_k2=peer_k2.astype(BF16),
        peer_u=peer_u.astype(BF16), peer_v=peer_v.astype(BF16),
        ln2_g=row(ln2_g), ln2_b=row(ln2_b), ple_w=ple_w.astype(BF16), ple_gate_w=ple_gate_w.astype(BF16),
        ln3_g=row(ln3_g), ln3_b=row(ln3_b))


def _tile(n, pref):
    t = min(n, pref)
    while n % t:
        t -= 8
    return t


def _channel_mix(x2d, o_mla, raw, g, bonus, p_emb, w):
    n = x2d.shape[0]
    x1 = _mix_out(x2d, o_mla, raw, g, bonus, w["gn_g"], w["gn_b"], w["e"], w["et"], w["wo1"], w["wo2"],
                  w["ln1_g"], w["ln1_b"], _tile(n, 256))
    i1, i2, gate = _peer_route(x1, w["peer_wq"], w["peer_k1"], w["peer_k2"], _tile(n, 256))
    wt = _peer_weights(i1, i2, gate, _tile(n, 128))
    y = _peer_dense(x1, wt, w["peer_u"], w["peer_v"], _tile(n, 512), 8)
    return _ple(x1, y, p_emb, w["ln2_g"], w["ln2_b"], w["ple_gate_w"], w["ple_w"], w["ln3_g"], w["ln3_b"],
                _tile(n, 256))


def _layer(xp, xs, pp, ps, past_len, cache_ckv, cache_krope, page_table, wkv0, shift0, w):
    b, t, _ = xp.shape
    db, ts, _ = xs.shape
    assert ts == 1, "the sample path handles one new token per sequence"
    n = b * t
    r3 = lambda a: a.reshape(b, t, -1)
    f2 = lambda a: a.reshape(n, -1)

    xp2 = xp.reshape(n, D_MODEL)
    cq, sq, ck, sk = (jnp.tile(a, (b, 1)) for a in _rope_tables(jnp.arange(t)))
    qc, kc, ckv_p, kr_p, v = _mla_proj(xp2, w["w_mla"], w["kv_g"], cq, sq, ck, sk,
                                       (w["w_uk2"], w["w_uv2"]), True, _tile(n, 256))
    o_mla_p = _mla_prompt_attention(r3(qc), r3(kc), r3(v), _tile(t, 512), _tile(t, 512))
    r, dec, k, vv, nkk, bb, g_p, bonus_p, last_p = _rwkv_proj_prompt(xp, w["rw"], _tile(t, 256))
    s0 = jnp.zeros((b, RW_HEADS, RW_HEAD, RW_HEAD), F32)

    xs2 = xs.reshape(db, D_MODEL)
    cq, sq, ck, sk = (jnp.tile(a, (db, 1)) for a in _rope_tables(jnp.full((1,), past_len)))
    ql, qr_s, ckv_s, kr_s = _mla_proj(xs2, w["w_mla"], w["kv_g"], cq, sq, ck, sk, (w["w_ukt"],), False,
                                      _tile(db, 128))
    decode_args = (page_table, ql.reshape(db, MLA_HEADS, KV_RANK), qr_s.reshape(db, MLA_HEADS, QK_ROPE),
                   ckv_s.reshape(db, 1, KV_RANK), kr_s.reshape(db, 1, QK_ROPE),
                   cache_ckv, jnp.swapaxes(cache_krope, 1, 2), math.gcd(page_table.shape[1], DECODE_PAGES))
    o_lat = _mla_sample_attention(*decode_args)
    raw_p, wkv_p = _rwkv_scan(s0, r, dec, k, vv, nkk, bb, w["e2"], b, _tile(t, 128))
    yp = _channel_mix(xp2, f2(o_mla_p), f2(raw_p), f2(g_p), f2(bonus_p), pp.reshape(n, -1), w)

    o_mla_s = _head_up(o_lat.reshape(db, MLA_HEADS * KV_RANK), w["w_uv2"])
    r, dec, k, vv, nkk, bb, g_s, bonus_s, last_s = _rwkv_proj_sample(xs2, shift0, w["rw"])
    s3 = lambda a: a.reshape(db, 1, RW_WIDTH)
    raw_s, wkv_s = _rwkv_scan(wkv0, s3(r), s3(dec), s3(k), s3(vv), s3(nkk), s3(bb), w["e2"], _tile(db, 8), 1)
    ys = _channel_mix(xs2, o_mla_s, raw_s.reshape(db, RW_WIDTH), g_s, bonus_s, ps.reshape(db, -1), w)
    return ((yp.reshape(b, t, D_MODEL), r3(ckv_p), r3(kr_p), wkv_p, last_p.reshape(b, RW_COLS)),
            (ys.reshape(db, 1, D_MODEL), ckv_s.reshape(db, 1, KV_RANK), kr_s.reshape(db, 1, QK_ROPE), wkv_s, last_s))


def kernel(x_prompt, x_sample, p_prompt, p_sample, cache_ckv, cache_krope, state_wkv, state_shift, page_table, w_in, kv_norm_g, w_uk, w_uv, rw_mu, rw_w0, rw_w_up, rw_a0, rw_a_up, rw_g_up, rw_k_k, rw_k_a, rw_r_k, rw_gn_g, rw_gn_b, w_o, ln1_g, ln1_b, peer_wq, peer_k1, peer_k2, peer_u, peer_v, ln2_g, ln2_b, ple_w, ple_gate_w, ln3_g, ln3_b):
    layer_params = (w_in, kv_norm_g, w_uk, w_uv, rw_mu, rw_w0, rw_w_up, rw_a0, rw_a_up, rw_g_up,
                    rw_k_k, rw_k_a, rw_r_k, rw_gn_g, rw_gn_b, w_o, ln1_g, ln1_b,
                    peer_wq, peer_k1, peer_k2, peer_u, peer_v, ln2_g, ln2_b,
                    ple_w, ple_gate_w, ln3_g, ln3_b)
    depth = w_in.shape[0]
    past_len = page_table.shape[1] * cache_ckv.shape[2]
    xp, xs = x_prompt, x_sample
    outs_p, outs_s = [], []
    for i in range(depth):
        w = _prepare_weights(*(a[i] for a in layer_params))
        (xp, *rest_p), (xs, *rest_s) = _layer(xp, xs, p_prompt[i], p_sample[i], past_len, cache_ckv[i],
                                              cache_krope[i], page_table, state_wkv[i], state_shift[i], w)
        outs_p.append(rest_p)
        outs_s.append(rest_s)
    stack = lambda outs, j: jnp.stack([o[j] for o in outs])
    return (xp, xs, stack(outs_p, 0), stack(outs_p, 1), stack(outs_p, 2), stack(outs_p, 3),
            stack(outs_s, 0), stack(outs_s, 1), stack(outs_s, 2), stack(outs_s, 3))
```

```python
import functools
import math

import jax
import jax.numpy as jnp
import numpy as np
from jax import lax
from jax.experimental import pallas as pl
from jax.experimental.pallas import tpu as pltpu

F32 = jnp.float32
BF16 = jnp.bfloat16

D_MODEL = 2048
MLA_HEADS = 8
QK_NOPE = 128
QK_ROPE = 64
QK_PAD = 256
V_HEAD = 128
KV_RANK = 512
ROPE_THETA = 10000.0
RW_HEADS = 16
RW_HEAD = 64
RW_WIDTH = RW_HEADS * RW_HEAD
DECAY_LORA = 64
AAA_LORA = 64
GATE_LORA = 160
RW_COLS = 3 * RW_WIDTH + DECAY_LORA + AAA_LORA + GATE_LORA
Q_COLS = MLA_HEADS * (QK_NOPE + QK_ROPE)
MLA_COLS = Q_COLS + KV_RANK + QK_ROPE
PEER_HEADS = 8
N_KEYS = 128
PK_DIM = 256
PK_HALF = PK_DIM // 2
PK_TOPK = 16
KEY_BITS = 7
TOPK_BITS = 4
PLE_DIM = 256
DEPTH = 1
ALPHA = (2 * DEPTH) ** 0.25
LN_EPS = 1e-5
RMS_EPS = 1e-6
GN_EPS = 64e-5
ATTN_SCALE = (QK_NOPE + QK_ROPE) ** -0.5

LANES = 128
SUBLANES = 8
RW_PAIRS = RW_HEADS // 2
DECODE_PAGES = 32
SCAN_DOT_PARTS = 2
PEER_TOKENS = 512
PEER_KEYS_PER_STEP = 8
SCAN_GROUP = SUBLANES
VMEM_LIMIT = 56 * 1024 * 1024

NT_DIMS = (((1,), (1,)), ((), ()))


def _cparams(sem, vmem=VMEM_LIMIT):
    return pltpu.CompilerParams(dimension_semantics=sem, vmem_limit_bytes=vmem)


def _resident(shape):
    nd = len(shape)
    return pl.BlockSpec(shape, lambda *_: (0,) * nd, pipeline_mode=pl.Buffered(1))


def _nt(a, b):
    return lax.dot_general(a, b, NT_DIMS, preferred_element_type=F32)


def _mm(a, b):
    return jnp.dot(a, b, preferred_element_type=F32)


def _dot01(x, e):
    h = x.astype(BF16)
    r = x - h.astype(F32)
    m = r.astype(BF16)
    l = (r - m.astype(F32)).astype(BF16)
    return _mm(h, e) + _mm(m, e) + _mm(l, e)


def _head_sum(x, e, et):
    return _dot01(_dot01(x, e), et)


def _sigmoid(x):
    return 1.0 / (1.0 + jnp.exp(-x))


def _layer_norm(x, g, b):
    mu = jnp.mean(x, axis=-1, keepdims=True)
    d = x - mu
    var = jnp.mean(d * d, axis=-1, keepdims=True)
    return d * lax.rsqrt(var + LN_EPS) * g + b


def _mla_proj_common(x_ref, w_ref, g_ref, cq_ref, sq_ref, ck_ref, sk_ref):
    x = x_ref[...].astype(BF16)
    res = _mm(x, w_ref[...])
    nq = MLA_HEADS * QK_NOPE
    rq = MLA_HEADS * QK_ROPE
    qn = res[:, :nq]
    qr = res[:, nq:nq + rq] * cq_ref[...] + res[:, nq + rq:nq + 2 * rq] * sq_ref[...]
    o = nq + 2 * rq
    c = res[:, o:o + KV_RANK]
    ckv = c * lax.rsqrt(jnp.mean(c * c, axis=-1, keepdims=True) + RMS_EPS) * g_ref[...]
    o += KV_RANK
    kr = res[:, o:o + QK_ROPE] * ck_ref[...] + res[:, o + QK_ROPE:o + 2 * QK_ROPE] * sk_ref[...]
    return qn, qr, ckv, kr


def _mla_proj_prompt_kernel(x_ref, w_ref, g_ref, cq_ref, sq_ref, ck_ref, sk_ref, wuk_ref, wuv_ref,
                            qc_o, kc_o, ckv_o, kr_o, v_o):
    qn, qr, ckv, kr = _mla_proj_common(x_ref, w_ref, g_ref, cq_ref, sq_ref, ck_ref, sk_ref)
    ckv_o[...] = ckv
    kr_o[...] = kr
    cb = ckv.astype(BF16)
    kn = _mm(cb, wuk_ref[...])
    v_o[...] = _mm(cb, wuv_ref[...]).astype(BF16)
    tm = qn.shape[0]
    low = lax.broadcasted_iota(jnp.int32, (tm, LANES), 1) < QK_ROPE
    kr_pad = jnp.concatenate([kr, jnp.zeros_like(kr)], axis=1).astype(BF16)
    for h in range(MLA_HEADS):
        blk = qr[:, (h // 2) * LANES:(h // 2 + 1) * LANES]
        if h % 2:
            blk = pltpu.roll(blk, QK_ROPE, axis=1)
        o = h * QK_PAD
        qc_o[:, o:o + QK_NOPE] = qn[:, h * QK_NOPE:(h + 1) * QK_NOPE].astype(BF16)
        qc_o[:, o + QK_NOPE:o + QK_PAD] = jnp.where(low, blk, 0.0).astype(BF16)
        kc_o[:, o:o + QK_NOPE] = kn[:, h * QK_NOPE:(h + 1) * QK_NOPE].astype(BF16)
        kc_o[:, o + QK_NOPE:o + QK_PAD] = kr_pad


def _mla_proj_sample_kernel(x_ref, w_ref, g_ref, cq_ref, sq_ref, ck_ref, sk_ref, wukt_ref,
                            ql_o, qr_o, ckv_o, kr_o):
    qn, qr, ckv, kr = _mla_proj_common(x_ref, w_ref, g_ref, cq_ref, sq_ref, ck_ref, sk_ref)
    qr_o[...] = qr.astype(BF16)
    ckv_o[...] = ckv
    kr_o[...] = kr
    for h in range(MLA_HEADS):
        qh = qn[:, h * QK_NOPE:(h + 1) * QK_NOPE].astype(BF16)
        ql_o[:, h * KV_RANK:(h + 1) * KV_RANK] = _mm(qh, wukt_ref[h]).astype(BF16)


def _mla_proj(x2d, w_all, kv_g, cq, sq, ck, sk, up_weights, prompt, tm):
    n = x2d.shape[0]
    wcols = w_all.shape[1]
    nq, rq = MLA_HEADS * QK_NOPE, MLA_HEADS * QK_ROPE
    row = lambda c: pl.BlockSpec((tm, c), lambda i: (i, 0))
    in_specs = [row(D_MODEL), _resident((D_MODEL, wcols)), _resident((1, KV_RANK)),
                row(rq), row(rq), row(QK_ROPE), row(QK_ROPE)]
    if prompt:
        kern = _mla_proj_prompt_kernel
        in_specs += [_resident((KV_RANK, nq)), _resident((KV_RANK, nq))]
        pad = MLA_HEADS * QK_PAD
        out_shape = [jax.ShapeDtypeStruct((n, pad), BF16), jax.ShapeDtypeStruct((n, pad), BF16),
                     jax.ShapeDtypeStruct((n, KV_RANK), F32), jax.ShapeDtypeStruct((n, QK_ROPE), F32),
                     jax.ShapeDtypeStruct((n, nq), BF16)]
        out_specs = [row(pad), row(pad), row(KV_RANK), row(QK_ROPE), row(nq)]
    else:
        kern = _mla_proj_sample_kernel
        in_specs += [_resident((MLA_HEADS, QK_NOPE, KV_RANK))]
        out_shape = [jax.ShapeDtypeStruct((n, MLA_HEADS * KV_RANK), BF16), jax.ShapeDtypeStruct((n, rq), BF16),
                     jax.ShapeDtypeStruct((n, KV_RANK), F32), jax.ShapeDtypeStruct((n, QK_ROPE), F32)]
        out_specs = [row(MLA_HEADS * KV_RANK), row(rq), row(KV_RANK), row(QK_ROPE)]
    return pl.pallas_call(
        kern, grid=(n // tm,), in_specs=in_specs, out_specs=out_specs, out_shape=out_shape,
        compiler_params=_cparams(("parallel",)), name="mla_proj",
    )(x2d, w_all, kv_g, cq, sq, ck, sk, *up_weights)


def _flash_kernel(q_ref, k_ref, v_ref, o_ref, m_ref, l_ref, acc_ref, s_ref, p_ref, *, tq, tk, nk):
    i = pl.program_id(1)
    j = pl.program_id(2)

    @pl.when(j == 0)
    def _():
        m_ref[...] = jnp.full(m_ref.shape, -jnp.inf, F32)
        l_ref[...] = jnp.zeros(l_ref.shape, F32)
        acc_ref[...] = jnp.zeros(acc_ref.shape, F32)

    def sweep(masked):
        if masked:
            qpos = i * tq + lax.broadcasted_iota(jnp.int32, (tq, tk), 0)
            kpos = j * tk + lax.broadcasted_iota(jnp.int32, (tq, tk), 1)
            visible = kpos <= qpos
        for h in range(MLA_HEADS):
            hs = slice(h * QK_PAD, (h + 1) * QK_PAD)
            s = _nt(q_ref[0, :, hs], k_ref[0, :, hs]) * ATTN_SCALE
            s_ref[h] = jnp.where(visible, s, -jnp.inf) if masked else s
        alphas = []
        for h in range(MLA_HEADS):
            s = s_ref[h]
            m_prev = m_ref[h]
            m_new = jnp.maximum(m_prev, jnp.max(s, axis=1, keepdims=True))
            alpha = jnp.exp(m_prev - m_new)
            p = jnp.exp(s - m_new)
            l_ref[h] = alpha * l_ref[h] + jnp.sum(p, axis=1, keepdims=True)
            m_ref[h] = m_new
            p_ref[h] = p.astype(BF16)
            alphas.append(alpha)
        for h in range(MLA_HEADS):
            acc_ref[h] = alphas[h] * acc_ref[h] + _mm(p_ref[h], v_ref[0, :, h * V_HEAD:(h + 1) * V_HEAD])

    first_q, last_q = i * tq, i * tq + tq - 1
    first_k, last_k = j * tk, j * tk + tk - 1

    @pl.when(last_k <= first_q)
    def _():
        sweep(False)

    @pl.when((last_k > first_q) & (first_k <= last_q))
    def _():
        sweep(True)

    @pl.when(j == nk - 1)
    def _():
        for h in range(MLA_HEADS):
            o_ref[0, :, h * V_HEAD:(h + 1) * V_HEAD] = (acc_ref[h] / l_ref[h]).astype(o_ref.dtype)


def _mla_prompt_attention(q, k, v, tq, tk):
    b, s, _ = q.shape
    nq, nk = s // tq, s // tk
    last = lambda i: (i * tq + tq - 1) // tk
    qspec = lambda c: pl.BlockSpec((1, tq, c), lambda bb, i, j: (bb, i, 0))
    kspec = lambda c: pl.BlockSpec((1, tk, c), lambda bb, i, j: (bb, jnp.minimum(j, last(i)), 0))
    return pl.pallas_call(
        functools.partial(_flash_kernel, tq=tq, tk=tk, nk=nk),
        grid=(b, nq, nk),
        in_specs=[qspec(q.shape[2]), kspec(k.shape[2]), kspec(v.shape[2])],
        out_specs=qspec(v.shape[2]),
        out_shape=jax.ShapeDtypeStruct(v.shape, BF16),
        scratch_shapes=[pltpu.VMEM((MLA_HEADS, tq, 1), F32), pltpu.VMEM((MLA_HEADS, tq, 1), F32),
                        pltpu.VMEM((MLA_HEADS, tq, V_HEAD), F32),
                        pltpu.VMEM((MLA_HEADS, tq, tk), F32), pltpu.VMEM((MLA_HEADS, tq, tk), BF16)],
        compiler_params=_cparams(("parallel", "parallel", "arbitrary")), name="mla_prompt_attention",
    )(q, k, v)


def _decode_kernel(pt_ref, ql_ref, qr_ref, cn_ref, kn_ref, ckv_hbm, krt_hbm, o_ref,
                   m_ref, l_ref, acc_ref, cbuf, kbuf, csem, ksem, *, pages, ng, n_steps):
    step = pl.program_id(0) * ng + pl.program_id(1)
    slot = step % 2

    def copies(s, sl):
        out = []
        for j in range(pages):
            pg = pt_ref[s * pages + j]
            out.append(pltpu.make_async_copy(ckv_hbm.at[pg], cbuf.at[sl, j], csem.at[sl]))
            out.append(pltpu.make_async_copy(krt_hbm.at[pg], kbuf.at[sl, j], ksem.at[sl]))
        return out

    @pl.when(step == 0)
    def _():
        for cp in copies(step, slot):
            cp.start()

    @pl.when(step + 1 < n_steps)
    def _():
        for cp in copies(step + 1, 1 - slot):
            cp.start()

    for cp in copies(step, slot):
        cp.wait()
    _decode_init(m_ref, l_ref, acc_ref)
    _decode_sweep(ql_ref, qr_ref, [cbuf.at[slot, j] for j in range(pages)],
                  [kbuf.at[slot, j] for j in range(pages)], m_ref, l_ref, acc_ref)
    _decode_final(ql_ref, qr_ref, cn_ref, kn_ref, o_ref, m_ref, l_ref, acc_ref, ng)


def _decode_init(m_ref, l_ref, acc_ref):
    @pl.when(pl.program_id(1) == 0)
    def _():
        m_ref[...] = jnp.full(m_ref.shape, -jnp.inf, F32)
        l_ref[...] = jnp.zeros(l_ref.shape, F32)
        acc_ref[...] = jnp.zeros(acc_ref.shape, F32)


def _decode_sweep(ql_ref, qr_ref, c_refs, k_refs, m_ref, l_ref, acc_ref):
    pages = len(c_refs)
    ql = ql_ref[0]
    qr = qr_ref[0]
    cs, ss = [], []
    for i in range(pages):
        c = c_refs[i][...].astype(BF16)
        k = k_refs[i][...].astype(BF16)
        cs.append(c)
        ss.append(_nt(ql, c) + _mm(qr, k))
    s = jnp.concatenate(ss, axis=1) * ATTN_SCALE
    m_prev = m_ref[...]
    m_new = jnp.maximum(m_prev, jnp.max(s, axis=1, keepdims=True))
    alpha = jnp.exp(m_prev - m_new)
    p = jnp.exp(s - m_new)
    pb = p.astype(BF16)
    page = cs[0].shape[0]
    pv = _mm(pb[:, :page], cs[0])
    for i in range(1, pages):
        pv = pv + _mm(pb[:, i * page:(i + 1) * page], cs[i])
    m_ref[...] = m_new
    l_ref[...] = alpha * l_ref[...] + jnp.sum(p, axis=1, keepdims=True)
    acc_ref[...] = alpha * acc_ref[...] + pv


def _decode_final(ql_ref, qr_ref, cn_ref, kn_ref, o_ref, m_ref, l_ref, acc_ref, ng):
    @pl.when(pl.program_id(1) == ng - 1)
    def _():
        ql, qr = ql_ref[0], qr_ref[0]
        m_new = m_ref[...]
        cn = cn_ref[0].astype(BF16).astype(F32)
        kn = kn_ref[0].astype(BF16).astype(F32)
        s_self = (jnp.sum(ql.astype(F32) * cn, axis=1, keepdims=True)
                  + jnp.sum(qr.astype(F32) * kn, axis=1, keepdims=True)) * ATTN_SCALE
        m_f = jnp.maximum(m_new, s_self)
        a_f = jnp.exp(m_new - m_f)
        p_self = jnp.exp(s_self - m_f)
        l_f = a_f * l_ref[...] + p_self
        acc_f = a_f * acc_ref[...] + p_self.astype(BF16).astype(F32) * cn
        o_ref[0] = acc_f / l_f


def _mla_sample_attention(page_table, q_lat, q_rope, c_new, kr_new, cache_ckv, cache_krope_t, pages):
    db, n_pages = page_table.shape
    _, page, _ = cache_ckv.shape
    ng = n_pages // pages
    pt = page_table.reshape(-1)
    per_seq = lambda shape: pl.BlockSpec((1,) + shape, lambda b, g, pt_: (b, 0, 0))
    in_hbm = pl.BlockSpec(memory_space=pl.ANY)
    grid_spec = pltpu.PrefetchScalarGridSpec(
        num_scalar_prefetch=1, grid=(db, ng),
        in_specs=[per_seq((MLA_HEADS, KV_RANK)), per_seq((MLA_HEADS, QK_ROPE)),
                  per_seq((1, KV_RANK)), per_seq((1, QK_ROPE)), in_hbm, in_hbm],
        out_specs=per_seq((MLA_HEADS, KV_RANK)),
        scratch_shapes=[pltpu.VMEM((MLA_HEADS, 1), F32), pltpu.VMEM((MLA_HEADS, 1), F32),
                        pltpu.VMEM((MLA_HEADS, KV_RANK), F32),
                        pltpu.VMEM((2, pages, page, KV_RANK), F32), pltpu.VMEM((2, pages, QK_ROPE, page), F32),
                        pltpu.SemaphoreType.DMA((2,)), pltpu.SemaphoreType.DMA((2,))])
    return pl.pallas_call(
        functools.partial(_decode_kernel, pages=pages, ng=ng, n_steps=db * ng),
        grid_spec=grid_spec,
        out_shape=jax.ShapeDtypeStruct((db, MLA_HEADS, KV_RANK), F32),
        compiler_params=_cparams(("arbitrary", "arbitrary")), name="mla_sample_attention",
    )(pt, q_lat, q_rope, c_new, kr_new, cache_ckv, cache_krope_t)


def _head_up_kernel(o_ref, w_ref, out_ref):
    out_ref[...] = _mm(o_ref[...].astype(BF16), w_ref[...]).astype(out_ref.dtype)


def _head_up(o_lat2d, w_uv2):
    n = o_lat2d.shape[0]
    return pl.pallas_call(
        _head_up_kernel, grid=(MLA_HEADS,),
        in_specs=[pl.BlockSpec((n, KV_RANK), lambda h: (0, h)), pl.BlockSpec((KV_RANK, V_HEAD), lambda h: (0, h))],
        out_specs=pl.BlockSpec((n, V_HEAD), lambda h: (0, h)),
        out_shape=jax.ShapeDtypeStruct((n, MLA_HEADS * V_HEAD), BF16),
        compiler_params=_cparams(("parallel",)), name="mla_head_up",
    )(o_lat2d, w_uv2)


def _rwkv_terms(p, prev, mu_ref, w0_ref, wup_ref, a0_ref, aup_ref, gup_ref, kk_ref, ka_ref, rk_ref,
                e_ref, et_ref, outs):
    r_o, w_o, k_o, v_o, n_o, b_o, g_o, bonus_o = outs
    xs = p + mu_ref[...] * (prev - p)
    w3 = 3 * RW_WIDTH
    r = xs[:, :RW_WIDTH]
    k0 = xs[:, RW_WIDTH:2 * RW_WIDTH]
    v = xs[:, 2 * RW_WIDTH:w3]
    xw = xs[:, w3:w3 + DECAY_LORA]
    xa = xs[:, w3 + DECAY_LORA:w3 + DECAY_LORA + AAA_LORA]
    xg = xs[:, w3 + DECAY_LORA + AAA_LORA:]
    y = w0_ref[...] + _mm(jnp.tanh(xw).astype(BF16), wup_ref[...])
    decay = jnp.exp(-math.exp(-0.5) * _sigmoid(y))
    a = _sigmoid(a0_ref[...] + _mm(xa.astype(BF16), aup_ref[...]))
    g = _mm(_sigmoid(xg).astype(BF16), gup_ref[...])
    kk = k0 * kk_ref[...]
    e, et = e_ref[...], et_ref[...]
    kk = kk * lax.rsqrt(jnp.maximum(_head_sum(kk * kk, e, et), 1e-24))
    k = k0 * (1.0 + (a - 1.0) * ka_ref[...])
    r_o[...] = r
    w_o[...] = decay
    k_o[...] = k
    v_o[...] = v
    n_o[...] = -kk
    b_o[...] = kk * a
    g_o[...] = g
    bonus_o[...] = _head_sum(r * k * rk_ref[...], e, et) * v


def _rwkv_proj_prompt_kernel(x_ref, w_ref, mu_ref, w0_ref, wup_ref, a0_ref, aup_ref, gup_ref,
                             kk_ref, ka_ref, rk_ref, e_ref, et_ref,
                             r_o, w_o, k_o, v_o, n_o, b_o, g_o, bonus_o, last_o, carry_ref):
    tm = x_ref.shape[1]

    @pl.when(pl.program_id(1) == 0)
    def _():
        carry_ref[...] = jnp.zeros(carry_ref.shape, F32)

    p = _mm(x_ref[0].astype(BF16), w_ref[...])
    rows = lax.broadcasted_iota(jnp.int32, p.shape, 0)
    prev = jnp.where(rows == 0, carry_ref[...], pltpu.roll(p, 1, axis=0))
    last = p[tm - 1:tm, :]
    carry_ref[...] = last
    last_o[0] = last
    outs = tuple(o.at[0] for o in (r_o, w_o, k_o, v_o, n_o, b_o, g_o, bonus_o))
    _rwkv_terms(p, prev, mu_ref, w0_ref, wup_ref, a0_ref, aup_ref, gup_ref, kk_ref, ka_ref, rk_ref,
                e_ref, et_ref, outs)


def _rwkv_proj_sample_kernel(x_ref, prev_ref, w_ref, mu_ref, w0_ref, wup_ref, a0_ref, aup_ref, gup_ref,
                             kk_ref, ka_ref, rk_ref, e_ref, et_ref,
                             r_o, w_o, k_o, v_o, n_o, b_o, g_o, bonus_o, last_o):
    p = _mm(x_ref[...].astype(BF16), w_ref[...])
    last_o[...] = p
    _rwkv_terms(p, prev_ref[...], mu_ref, w0_ref, wup_ref, a0_ref, aup_ref, gup_ref, kk_ref, ka_ref, rk_ref,
                e_ref, et_ref, (r_o, w_o, k_o, v_o, n_o, b_o, g_o, bonus_o))


def _rwkv_weight_specs():
    vec = lambda c: _resident((1, c))
    return [_resident((D_MODEL, RW_COLS)), vec(RW_COLS), vec(RW_WIDTH), _resident((DECAY_LORA, RW_WIDTH)),
            vec(RW_WIDTH), _resident((AAA_LORA, RW_WIDTH)), _resident((GATE_LORA, RW_WIDTH)),
            vec(RW_WIDTH), vec(RW_WIDTH), vec(RW_WIDTH),
            _resident((RW_WIDTH, LANES)), _resident((LANES, RW_WIDTH))]


def _rwkv_proj_prompt(x, rw_weights, tm):
    b, t, _ = x.shape
    tile = lambda c: pl.BlockSpec((1, tm, c), lambda bb, i: (bb, i, 0))
    wide = jax.ShapeDtypeStruct((b, t, RW_WIDTH), F32)
    return pl.pallas_call(
        _rwkv_proj_prompt_kernel, grid=(b, t // tm),
        in_specs=[tile(D_MODEL)] + _rwkv_weight_specs(),
        out_specs=[tile(RW_WIDTH)] * 8 + [pl.BlockSpec((1, 1, RW_COLS), lambda bb, i: (bb, 0, 0))],
        out_shape=[wide] * 8 + [jax.ShapeDtypeStruct((b, 1, RW_COLS), F32)],
        scratch_shapes=[pltpu.VMEM((1, RW_COLS), F32)],
        compiler_params=_cparams(("parallel", "arbitrary")), name="rwkv_proj_prompt",
    )(x, *rw_weights)


def _rwkv_proj_sample(x2d, prev, rw_weights):
    n = x2d.shape[0]
    full = lambda c: pl.BlockSpec((n, c), lambda i: (0, 0))
    wide = jax.ShapeDtypeStruct((n, RW_WIDTH), F32)
    return pl.pallas_call(
        _rwkv_proj_sample_kernel, grid=(1,),
        in_specs=[full(D_MODEL), full(RW_COLS)] + _rwkv_weight_specs(),
        out_specs=[full(RW_WIDTH)] * 8 + [full(RW_COLS)],
        out_shape=[wide] * 8 + [jax.ShapeDtypeStruct((n, RW_COLS), F32)],
        compiler_params=_cparams(("arbitrary",)), name="rwkv_proj_sample",
    )(x2d, prev, *rw_weights)


def _scan_load_state(s0_ref, state_ref):
    for b in range(state_ref.shape[0]):
        for p in range(RW_PAIRS):
            state_ref[b, p] = jnp.concatenate([s0_ref[b, 2 * p], s0_ref[b, 2 * p + 1]], axis=1)


def _scan_store_state(st_ref, state_ref):
    for b in range(state_ref.shape[0]):
        for p in range(RW_PAIRS):
            s = state_ref[b, p]
            st_ref[b, 2 * p] = s[:, :RW_HEAD]
            st_ref[b, 2 * p + 1] = s[:, RW_HEAD:]


def _scan_steps(ins, e2_ref, eo_ref, vt_ref, ev_ref, out_ref, state_ref, vcol_ref, group):
    nb = state_ref.shape[0]
    chains = [(b, p) for b in range(nb) for p in range(RW_PAIRS)]
    lane = lax.broadcasted_iota(jnp.int32, (RW_HEAD, LANES), 1)
    sub = lax.broadcasted_iota(jnp.int32, (RW_HEAD, LANES), 0)
    diag = (lane & (RW_HEAD - 1)) == sub
    e2 = e2_ref[...]

    def pieces(x):
        hi = x.astype(BF16)
        return jnp.concatenate([hi, (x - hi.astype(F32)).astype(BF16)], axis=1)

    def head_sums(xs):
        out = []
        per = max(len(xs) // SCAN_DOT_PARTS, 1)
        for q in range(0, len(xs), per):
            res = _mm(jnp.concatenate([pieces(x) for x in xs[q:q + per]], axis=0), e2)
            out += [res[i * RW_HEAD:(i + 1) * RW_HEAD] for i in range(per)]
        return out

    def head_sums_out(xs):
        eo = eo_ref[...]
        out = []
        per = max(len(xs) // SCAN_DOT_PARTS, 2)
        for q in range(0, len(xs), per):
            res = _mm(jnp.concatenate([jnp.concatenate([xs[i].astype(BF16), xs[i + 1].astype(BF16)], axis=1)
                                       for i in range(q, q + per, 2)], axis=0), eo)
            for i in range(per // 2):
                blk = res[i * RW_HEAD:(i + 1) * RW_HEAD]
                out += [blk[:, :LANES], blk[:, LANES:]]
        return out

    def steps(tg):
        t0 = tg * group if isinstance(tg, int) else pl.multiple_of(tg * group, group)
        rows = [[ref[b, pl.ds(t0, group), p * LANES:(p + 1) * LANES] for ref in ins] for b, p in chains]
        outs = [[] for _ in chains]
        if group > 1:
            lhs = [pieces(jnp.where((lane % RW_HEADS) // 2 == p, vt_ref[b, tg], 0.0)) for b, p in chains]
            vcol_ref[...] = _mm(jnp.concatenate(lhs, axis=0), ev_ref[...])
        for i in range(group):
            r, w, k, v, n, bb = ([x[j][i:i + 1, :] for x in rows] for j in range(len(ins)))
            if group > 1:
                v_cols = [vcol_ref[ci * RW_HEAD:(ci + 1) * RW_HEAD, i * LANES:(i + 1) * LANES]
                          for ci in range(len(chains))]
            else:
                v_cols = head_sums([jnp.where(diag, vi, 0.0) for vi in v])
            sa = head_sums([state_ref[b, p] * n[ci] for ci, (b, p) in enumerate(chains)])
            new = []
            for ci, (b, p) in enumerate(chains):
                s = state_ref[b, p] * w[ci] + sa[ci] * bb[ci] + v_cols[ci] * k[ci]
                state_ref[b, p] = s
                new.append(s * r[ci])
            for ci, o_col in enumerate(head_sums_out(new)):
                outs[ci].append(jnp.sum(jnp.where(diag, o_col, 0.0), axis=0, keepdims=True))
        for ci, (b, p) in enumerate(chains):
            out_ref[b, pl.ds(t0, group), p * LANES:(p + 1) * LANES] = (
                outs[ci][0] if group == 1 else jnp.concatenate(outs[ci], axis=0))

    return steps


def _scan_loop(steps, n_groups):
    if n_groups == 1:
        steps(0)
    else:
        def body(tg, carry):
            steps(tg)
            return carry
        lax.fori_loop(0, n_groups, body, 0)


def _rwkv_scan_kernel(s0_ref, r_ref, w_ref, k_ref, v_ref, n_ref, b_ref, e2_ref, eo_ref, *rest, tc, nc):
    c = pl.program_id(1)
    group = SCAN_GROUP if tc % SCAN_GROUP == 0 else 1
    if group > 1:
        vt_ref, ev_ref, out_ref, st_ref, state_ref, vcol_ref = rest
    else:
        (out_ref, st_ref, state_ref), vt_ref, ev_ref, vcol_ref = rest, None, None, None

    @pl.when(c == 0)
    def _():
        _scan_load_state(s0_ref, state_ref)

    _scan_loop(_scan_steps((r_ref, w_ref, k_ref, v_ref, n_ref, b_ref), e2_ref, eo_ref, vt_ref, ev_ref,
                           out_ref, state_ref, vcol_ref, group), tc // group)

    @pl.when(c == nc - 1)
    def _():
        _scan_store_state(st_ref, state_ref)


def _scan_constants(e2):
    same_head = e2[:LANES]
    zero = jnp.zeros_like(same_head)
    eo = jnp.concatenate([jnp.concatenate([same_head, zero], axis=1),
                          jnp.concatenate([zero, same_head], axis=1)], axis=0)
    src = jnp.arange(2 * LANES) % LANES
    dst = jnp.arange(SCAN_GROUP * LANES)
    ev = ((src[:, None] // RW_HEADS == dst[None, :] // LANES)
          & (src[:, None] % 2 == (dst[None, :] % LANES) // RW_HEAD))
    return eo, ev.astype(BF16)


def _transposed_v(v):
    b, t, _ = v.shape
    return jnp.swapaxes(v.reshape(b, t // SCAN_GROUP, SCAN_GROUP * RW_HEADS, RW_HEAD), 2, 3)


def _rwkv_scan(s0, r, w, k, v, n, bb, e2, nb, tc):
    b, t, _ = r.shape
    nc = t // tc
    seq = pl.BlockSpec((nb, tc, RW_WIDTH), lambda i, c: (i, c, 0))
    st = pl.BlockSpec((nb, RW_HEADS, RW_HEAD, RW_HEAD), lambda i, c: (i, 0, 0, 0))
    eo, ev = _scan_constants(e2)
    in_specs = [st] + [seq] * 6 + [_resident(e2.shape), _resident(eo.shape)]
    args = [s0, r, w, k, v, n, bb, e2, eo]
    scratch = [pltpu.VMEM((nb, RW_PAIRS, RW_HEAD, LANES), F32)]
    if tc % SCAN_GROUP == 0:
        in_specs += [pl.BlockSpec((nb, tc // SCAN_GROUP, RW_HEAD, LANES), lambda i, c: (i, c, 0, 0)),
                     _resident(ev.shape)]
        args += [_transposed_v(v), ev]
        scratch.append(pltpu.VMEM((nb * RW_PAIRS * RW_HEAD, SCAN_GROUP * LANES), F32))
    return pl.pallas_call(
        functools.partial(_rwkv_scan_kernel, tc=tc, nc=nc),
        grid=(b // nb, nc),
        in_specs=in_specs,
        out_specs=[seq, st],
        out_shape=[jax.ShapeDtypeStruct((b, t, RW_WIDTH), F32),
                   jax.ShapeDtypeStruct((b, RW_HEADS, RW_HEAD, RW_HEAD), F32)],
        scratch_shapes=scratch,
        compiler_params=_cparams(("parallel", "arbitrary")), name="rwkv_scan",
    )(*args)


def _mix_out_kernel(x_ref, om_ref, raw_ref, g_ref, bonus_ref, gng_ref, gnb_ref, e_ref, et_ref,
                    wo1_ref, wo2_ref, lng_ref, lnb_ref, o_ref, ob_ref):
    e, et = e_ref[...], et_ref[...]
    raw = raw_ref[...]
    mu = _head_sum(raw, e, et) * (1.0 / RW_HEAD)
    d = raw - mu
    var = _head_sum(d * d, e, et) * (1.0 / RW_HEAD)
    o_rw = (d * lax.rsqrt(var + GN_EPS) * gng_ref[...] + gnb_ref[...] + bonus_ref[...]) * g_ref[...]
    mixed = _mm(om_ref[...], wo1_ref[...]) + _mm(o_rw.astype(BF16), wo2_ref[...])
    x1 = _layer_norm(ALPHA * x_ref[...] + mixed, lng_ref[...], lnb_ref[...])
    o_ref[...] = x1
    ob_ref[...] = x1.astype(BF16)


def _mix_out(x2d, o_mla, raw, g, bonus, gng, gnb, e, et, wo1, wo2, lng, lnb, tm):
    n = x2d.shape[0]
    row = lambda c: pl.BlockSpec((tm, c), lambda i: (i, 0))
    vec = lambda c: _resident((1, c))
    return pl.pallas_call(
        _mix_out_kernel, grid=(n // tm,),
        in_specs=[row(D_MODEL), row(RW_WIDTH), row(RW_WIDTH), row(RW_WIDTH), row(RW_WIDTH),
                  vec(RW_WIDTH), vec(RW_WIDTH), _resident((RW_WIDTH, LANES)), _resident((LANES, RW_WIDTH)),
                  _resident(wo1.shape), _resident(wo2.shape), vec(D_MODEL), vec(D_MODEL)],
        out_specs=[row(D_MODEL), row(D_MODEL)],
        out_shape=[jax.ShapeDtypeStruct((n, D_MODEL), F32), jax.ShapeDtypeStruct((n, D_MODEL), BF16)],
        compiler_params=_cparams(("parallel",)), name="mix_out_ln1",
    )(x2d, o_mla, raw, g, bonus, gng, gnb, e, et, wo1, wo2, lng, lnb)


def _peer_route_kernel(x_ref, wq_ref, k1_ref, k2_ref, pa_ref, pb_ref, flat_ref, i1_o, i2_o, g_o):
    tm = x_ref.shape[0]
    kk2 = PK_TOPK * PK_TOPK
    q = _mm(x_ref[...].astype(BF16), wq_ref[...])
    key_row = lax.broadcasted_iota(jnp.int32, (N_KEYS, tm), 0).astype(F32)
    rank_row = lax.broadcasted_iota(jnp.int32, (PK_TOPK, tm), 0)
    neg = -jnp.inf
    pa, pb = pa_ref[...], pb_ref[...]
    flat = jnp.concatenate([flat_ref[...]] * -(-tm // LANES), axis=1)[:, :tm]
    is_pair = flat < float(kk2)

    def pick(p, x):
        h = x.astype(BF16)
        r = x - h.astype(F32)
        m = r.astype(BF16)
        l = (r - m.astype(F32)).astype(BF16)
        return _mm(p, h) + _mm(p, m) + _mm(p, l)

    def top_keys(scores):
        def body(r, carry):
            hit = rank_row == r
            ms = [jnp.max(s, axis=0, keepdims=True) for s, _, _ in carry]
            ids = [jnp.min(jnp.where(s == m, key_row, float(N_KEYS)), axis=0, keepdims=True)
                   for (s, _, _), m in zip(carry, ms)]
            return tuple((jnp.where(key_row == idx, neg, s), jnp.where(hit, m, vals), jnp.where(hit, idx, idxs))
                         for (s, vals, idxs), m, idx in zip(carry, ms, ids))
        zero = jnp.zeros((PK_TOPK, tm), F32)
        res = lax.fori_loop(0, PK_TOPK, body, tuple((s, zero, zero) for s in scores))
        return [(vals, idxs) for _, vals, idxs in res]

    experts, gates = [], []
    for h in range(PEER_HEADS):
        q1 = q[:, h * PK_DIM:h * PK_DIM + PK_HALF].astype(BF16)
        q2 = q[:, h * PK_DIM + PK_HALF:(h + 1) * PK_DIM].astype(BF16)
        (v1, i1), (v2, i2) = top_keys([_nt(k1_ref[h], q1), _nt(k2_ref[h], q2)])
        cand = jnp.where(is_pair, pick(pa, v1) + pick(pb, v2), neg)
        expert = pick(pa, i1) * float(N_KEYS) + pick(pb, i2)

        def body(r, carry, expert=expert):
            cnd, top, den, e_a, p_a = carry
            m = jnp.max(cnd, axis=0, keepdims=True)
            idx = jnp.min(jnp.where(cnd == m, flat, float(kk2)), axis=0, keepdims=True)
            hit = flat == idx
            e = jnp.max(jnp.where(hit, expert, -1.0), axis=0, keepdims=True)
            cnd = jnp.where(hit, neg, cnd)
            top = jnp.where(r == 0, m, top)
            pr = jnp.exp(m - top)
            sel = rank_row == r
            return cnd, top, den + pr, jnp.where(sel, e, e_a), jnp.where(sel, pr, p_a)

        zrow = jnp.zeros((1, tm), F32)
        zk = jnp.zeros((PK_TOPK, tm), F32)
        _, _, den, e_h, p_h = lax.fori_loop(0, PK_TOPK, body, (cand, zrow, zrow, zk, zk))
        experts.append(e_h)
        gates.append(p_h / den)
    e_int = jnp.concatenate(experts, axis=0).T.astype(jnp.int32)
    i1_o[...] = e_int >> KEY_BITS
    i2_o[...] = e_int & (N_KEYS - 1)
    g_o[...] = jnp.concatenate(gates, axis=0).T


def _candidate_pairs():
    pairs = [(a, b) for a in range(PK_TOPK) for b in range(PK_TOPK // (a + 1))]
    rows = -(-len(pairs) // 16) * 16
    pa = np.zeros((rows, PK_TOPK), np.float32)
    pb = np.zeros((rows, PK_TOPK), np.float32)
    flat = np.full((rows, LANES), float(PK_TOPK * PK_TOPK), np.float32)
    for i, (a, b) in enumerate(pairs):
        pa[i, a] = 1.0
        pb[i, b] = 1.0
        flat[i, :] = a * PK_TOPK + b
    return jnp.asarray(pa, BF16), jnp.asarray(pb, BF16), jnp.asarray(flat)


def _peer_route(x2d, wq, k1, k2, tm):
    n = x2d.shape[0]
    slots = PEER_HEADS * PK_TOPK
    row = lambda c: pl.BlockSpec((tm, c), lambda i: (i, 0))
    pa, pb, flat = _candidate_pairs()
    return pl.pallas_call(
        _peer_route_kernel, grid=(n // tm,),
        in_specs=[row(D_MODEL), _resident(wq.shape), _resident(k1.shape), _resident(k2.shape),
                  _resident(pa.shape), _resident(pb.shape), _resident(flat.shape)],
        out_specs=[row(slots)] * 3,
        out_shape=[jax.ShapeDtypeStruct((n, slots), jnp.int32), jax.ShapeDtypeStruct((n, slots), jnp.int32),
                   jax.ShapeDtypeStruct((n, slots), F32)],
        compiler_params=_cparams(("parallel",)), name="peer_route",
    )(x2d, wq, k1, k2, pa, pb, flat)


def _peer_weights_kernel(i1_ref, i2_ref, g_ref, wt_ref):
    tm = i1_ref.shape[0]
    key = lax.broadcasted_iota(jnp.int32, (N_KEYS, PEER_HEADS * PK_TOPK), 0)

    def body(grp, carry):
        t0 = pl.multiple_of(grp * SUBLANES, SUBLANES)
        i1s = i1_ref[pl.ds(t0, SUBLANES), :]
        i2s = i2_ref[pl.ds(t0, SUBLANES), :]
        gs = g_ref[pl.ds(t0, SUBLANES), :]
        per_token = []
        for j in range(SUBLANES):
            g1 = jnp.where(key == i1s[j:j + 1, :], gs[j:j + 1, :], 0.0)
            hi = g1.astype(BF16)
            lo = (g1 - hi.astype(F32)).astype(BF16)
            g2 = jnp.where(key == i2s[j:j + 1, :], 1.0, 0.0).astype(BF16)
            per_token.append(_nt(jnp.concatenate([hi, lo], axis=1), jnp.concatenate([g2, g2], axis=1)))
        wt_ref[grp] = jnp.swapaxes(jnp.stack(per_token, axis=0), 0, 1)
        return carry

    lax.fori_loop(0, tm // SUBLANES, body, 0)


def _peer_weights(i1, i2, g, tm):
    n, slots = i1.shape
    row = pl.BlockSpec((tm, slots), lambda i: (i, 0))
    return pl.pallas_call(
        _peer_weights_kernel, grid=(n // tm,),
        in_specs=[row, row, row],
        out_specs=pl.BlockSpec((tm // SUBLANES, N_KEYS, SUBLANES, N_KEYS), lambda i: (i, 0, 0, 0)),
        out_shape=jax.ShapeDtypeStruct((n // SUBLANES, N_KEYS, SUBLANES, N_KEYS), F32),
        compiler_params=_cparams(("parallel",)), name="peer_weights",
    )(i1, i2, g)


def _peer_dense_kernel(x_ref, wt_ref, u_ref, v_ref, y_ref, z_ref, *, rows_per_step):
    c = pl.program_id(1)
    tt = x_ref.shape[0]

    @pl.when(c == 0)
    def _():
        y_ref[...] = jnp.zeros(y_ref.shape, F32)

    hid = _nt(x_ref[...], u_ref[...])
    for a in range(rows_per_step):
        h = hid[:, a * N_KEYS:(a + 1) * N_KEYS]
        w = wt_ref[:, a].reshape(tt, N_KEYS)
        gelu = 0.5 * h * (1.0 + lax.erf(h * math.sqrt(0.5)))
        z_ref[:, a * N_KEYS:(a + 1) * N_KEYS] = (w * gelu).astype(BF16)
    y_ref[...] += _mm(z_ref[...], v_ref[...])


def _peer_dense(x2d, wt, u, v, tt, rows_per_step):
    n = x2d.shape[0]
    ne = rows_per_step * N_KEYS
    return pl.pallas_call(
        functools.partial(_peer_dense_kernel, rows_per_step=rows_per_step),
        grid=(n // tt, N_KEYS // rows_per_step),
        in_specs=[pl.BlockSpec((tt, D_MODEL), lambda i, c: (i, 0)),
                  pl.BlockSpec((tt // SUBLANES, rows_per_step, SUBLANES, N_KEYS), lambda i, c: (i, c, 0, 0)),
                  pl.BlockSpec((ne, D_MODEL), lambda i, c: (c, 0)),
                  pl.BlockSpec((ne, D_MODEL), lambda i, c: (c, 0))],
        out_specs=pl.BlockSpec((tt, D_MODEL), lambda i, c: (i, 0)),
        out_shape=jax.ShapeDtypeStruct((n, D_MODEL), F32),
        scratch_shapes=[pltpu.VMEM((tt, ne), BF16)],
        compiler_params=_cparams(("parallel", "arbitrary")), name="peer_dense",
    )(x2d, wt, u, v)


def _ple_kernel(x1_ref, y_ref, p_ref, l2g_ref, l2b_ref, wg_ref, wp_ref, l3g_ref, l3b_ref, o_ref):
    x2 = _layer_norm(ALPHA * x1_ref[...] + y_ref[...], l2g_ref[...], l2b_ref[...])
    gate = _sigmoid(_mm(x2.astype(BF16), wg_ref[...]))
    ple = _mm(p_ref[...].astype(BF16), wp_ref[...]) * gate
    o_ref[...] = _layer_norm(ALPHA * x2 + ple, l3g_ref[...], l3b_ref[...])


def _ple(x1, y, p_emb, l2g, l2b, wg, wp, l3g, l3b, tm):
    n = x1.shape[0]
    row = lambda c: pl.BlockSpec((tm, c), lambda i: (i, 0))
    vec = _resident((1, D_MODEL))
    return pl.pallas_call(
        _ple_kernel, grid=(n // tm,),
        in_specs=[row(D_MODEL), row(D_MODEL), row(PLE_DIM), vec, vec, _resident(wg.shape), _resident(wp.shape), vec, vec],
        out_specs=row(D_MODEL),
        out_shape=jax.ShapeDtypeStruct((n, D_MODEL), F32),
        compiler_params=_cparams(("parallel",)), name="ple_ln2_ln3",
    )(x1, y, p_emb, l2g, l2b, wg, wp, l3g, l3b)


def _rope_tables(pos):
    inv = ROPE_THETA ** (-jnp.arange(0, QK_ROPE, 2, dtype=F32) / QK_ROPE)
    ang = pos.astype(F32)[:, None] * inv[None, :]
    cos, sin = jnp.cos(ang), jnp.sin(ang)
    ck = jnp.concatenate([cos, cos], axis=-1)
    sk = jnp.concatenate([-sin, sin], axis=-1)
    return jnp.tile(ck, (1, MLA_HEADS)), jnp.tile(sk, (1, MLA_HEADS)), ck, sk


def _prepare_weights(w_in, kv_norm_g, w_uk, w_uv, rw_mu, rw_w0, rw_w_up, rw_a0, rw_a_up, rw_g_up,
                     rw_k_k, rw_k_a, rw_r_k, rw_gn_g, rw_gn_b, w_o, ln1_g, ln1_b,
                     peer_wq, peer_k1, peer_k2, peer_u, peer_v, ln2_g, ln2_b,
                     ple_w, ple_gate_w, ln3_g, ln3_b):
    half = QK_ROPE // 2
    swap = lambda w: jnp.concatenate([w[..., half:], w[..., :half]], axis=-1)
    wq = w_in[:, :Q_COLS].reshape(D_MODEL, MLA_HEADS, QK_NOPE + QK_ROPE)
    wq_n = wq[:, :, :QK_NOPE].reshape(D_MODEL, -1)
    wq_r = wq[:, :, QK_NOPE:]
    w_c = w_in[:, Q_COLS:Q_COLS + KV_RANK]
    w_kr = w_in[:, Q_COLS + KV_RANK:MLA_COLS]
    w_mla = jnp.concatenate([wq_n, wq_r.reshape(D_MODEL, -1), swap(wq_r).reshape(D_MODEL, -1),
                             w_c, w_kr, swap(w_kr)], axis=1).astype(BF16)
    row = lambda a: a.reshape(1, -1)
    head_of_lane = jnp.arange(RW_WIDTH) // RW_HEAD
    e = (head_of_lane[:, None] == jnp.arange(LANES)[None, :]).astype(BF16)
    half_of = (jnp.arange(2 * LANES) % LANES) // RW_HEAD
    e2 = (half_of[:, None] == half_of[None, :LANES]).astype(BF16)
    return dict(
        e2=e2,
        w_mla=w_mla, kv_g=row(kv_norm_g),
        w_uk2=w_uk.reshape(KV_RANK, -1).astype(BF16), w_uv2=w_uv.reshape(KV_RANK, -1).astype(BF16),
        w_ukt=jnp.transpose(w_uk, (1, 2, 0)).astype(BF16),
        rw=(w_in[:, MLA_COLS:].astype(BF16), row(rw_mu), row(rw_w0), rw_w_up.astype(BF16), row(rw_a0),
            rw_a_up.astype(BF16), rw_g_up.astype(BF16), row(rw_k_k), row(rw_k_a), row(rw_r_k), e, e.T),
        gn_g=row(rw_gn_g), gn_b=row(rw_gn_b), e=e, et=e.T,
        wo1=w_o[:MLA_HEADS * V_HEAD].astype(BF16), wo2=w_o[MLA_HEADS * V_HEAD:].astype(BF16),
        ln1_g=row(ln1_g), ln1_b=row(ln1_b),
        peer_wq=peer_wq.astype(BF16), peer_k1=peer_k1.astype(BF16), peer_k2=peer_k2.astype(BF16),
        peer_u=peer_u.astype(BF16), peer_v=peer_v.astype(BF16),
        ln2_g=row(ln2_g), ln2_b=row(ln2_b), ple_w=ple_w.astype(BF16), ple_gate_w=ple_gate_w.astype(BF16),
        ln3_g=row(ln3_g), ln3_b=row(ln3_b))


def _tile(n, pref):
    t = min(n, pref)
    while n % t:
        t -= 8
    return t


def _channel_mix(x2d, o_mla, raw, g, bonus, p_emb, w):
    n = x2d.shape[0]
    x1, x1b = _mix_out(x2d, o_mla, raw, g, bonus, w["gn_g"], w["gn_b"], w["e"], w["et"], w["wo1"], w["wo2"],
                       w["ln1_g"], w["ln1_b"], _tile(n, 256))
    i1, i2, gate = _peer_route(x1b, w["peer_wq"], w["peer_k1"], w["peer_k2"], _tile(n, 256))
    wt = _peer_weights(i1, i2, gate, _tile(n, 128))
    y = _peer_dense(x1b, wt, w["peer_u"], w["peer_v"], _tile(n, PEER_TOKENS), PEER_KEYS_PER_STEP)
    return _ple(x1, y, p_emb, w["ln2_g"], w["ln2_b"], w["ple_gate_w"], w["ple_w"], w["ln3_g"], w["ln3_b"],
                _tile(n, 256))


def _layer(xp, xs, pp, ps, past_len, cache_ckv, cache_krope, page_table, wkv0, shift0, w):
    b, t, _ = xp.shape
    db, ts, _ = xs.shape
    assert ts == 1, "the sample path handles one new token per sequence"
    n = b * t
    r3 = lambda a: a.reshape(b, t, -1)
    f2 = lambda a: a.reshape(n, -1)

    xp2 = xp.reshape(n, D_MODEL)
    cq, sq, ck, sk = (jnp.tile(a, (b, 1)) for a in _rope_tables(jnp.arange(t)))
    qc, kc, ckv_p, kr_p, v = _mla_proj(xp2, w["w_mla"], w["kv_g"], cq, sq, ck, sk,
                                       (w["w_uk2"], w["w_uv2"]), True, _tile(n, 256))
    o_mla_p = _mla_prompt_attention(r3(qc), r3(kc), r3(v), _tile(t, 512), _tile(t, 512))
    r, dec, k, vv, nkk, bb, g_p, bonus_p, last_p = _rwkv_proj_prompt(xp, w["rw"], _tile(t, 256))
    s0 = jnp.zeros((b, RW_HEADS, RW_HEAD, RW_HEAD), F32)

    xs2 = xs.reshape(db, D_MODEL)
    cq, sq, ck, sk = (jnp.tile(a, (db, 1)) for a in _rope_tables(jnp.full((1,), past_len)))
    ql, qr_s, ckv_s, kr_s = _mla_proj(xs2, w["w_mla"], w["kv_g"], cq, sq, ck, sk, (w["w_ukt"],), False,
                                      _tile(db, 128))
    decode_args = (page_table, ql.reshape(db, MLA_HEADS, KV_RANK), qr_s.reshape(db, MLA_HEADS, QK_ROPE),
                   ckv_s.reshape(db, 1, KV_RANK), kr_s.reshape(db, 1, QK_ROPE),
                   cache_ckv, jnp.swapaxes(cache_krope, 1, 2), math.gcd(page_table.shape[1], DECODE_PAGES))
    o_lat = _mla_sample_attention(*decode_args)
    raw_p, wkv_p = _rwkv_scan(s0, r, dec, k, vv, nkk, bb, w["e2"], b, _tile(t, 128))
    yp = _channel_mix(xp2, f2(o_mla_p), f2(raw_p), f2(g_p), f2(bonus_p), pp.reshape(n, -1), w)

    o_mla_s = _head_up(o_lat.reshape(db, MLA_HEADS * KV_RANK), w["w_uv2"])
    r, dec, k, vv, nkk, bb, g_s, bonus_s, last_s = _rwkv_proj_sample(xs2, shift0, w["rw"])
    s3 = lambda a: a.reshape(db, 1, RW_WIDTH)
    raw_s, wkv_s = _rwkv_scan(wkv0, s3(r), s3(dec), s3(k), s3(vv), s3(nkk), s3(bb), w["e2"], _tile(db, 8), 1)
    ys = _channel_mix(xs2, o_mla_s, raw_s.reshape(db, RW_WIDTH), g_s, bonus_s, ps.reshape(db, -1), w)
    return ((yp.reshape(b, t, D_MODEL), r3(ckv_p), r3(kr_p), wkv_p, last_p.reshape(b, RW_COLS)),
            (ys.reshape(db, 1, D_MODEL), ckv_s.reshape(db, 1, KV_RANK), kr_s.reshape(db, 1, QK_ROPE), wkv_s, last_s))


def kernel(x_prompt, x_sample, p_prompt, p_sample, cache_ckv, cache_krope, state_wkv, state_shift, page_table, w_in, kv_norm_g, w_uk, w_uv, rw_mu, rw_w0, rw_w_up, rw_a0, rw_a_up, rw_g_up, rw_k_k, rw_k_a, rw_r_k, rw_gn_g, rw_gn_b, w_o, ln1_g, ln1_b, peer_wq, peer_k1, peer_k2, peer_u, peer_v, ln2_g, ln2_b, ple_w, ple_gate_w, ln3_g, ln3_b):
    layer_params = (w_in, kv_norm_g, w_uk, w_uv, rw_mu, rw_w0, rw_w_up, rw_a0, rw_a_up, rw_g_up,
                    rw_k_k, rw_k_a, rw_r_k, rw_gn_g, rw_gn_b, w_o, ln1_g, ln1_b,
                    peer_wq, peer_k1, peer_k2, peer_u, peer_v, ln2_g, ln2_b,
                    ple_w, ple_gate_w, ln3_g, ln3_b)
    depth = w_in.shape[0]
    past_len = page_table.shape[1] * cache_ckv.shape[2]
    xp, xs = x_prompt, x_sample
    outs_p, outs_s = [], []
    for i in range(depth):
        w = _prepare_weights(*(a[i] for a in layer_params))
        (xp, *rest_p), (xs, *rest_s) = _layer(xp, xs, p_prompt[i], p_sample[i], past_len, cache_ckv[i],
                                              cache_krope[i], page_table, state_wkv[i], state_shift[i], w)
        outs_p.append(rest_p)
        outs_s.append(rest_s)
    stack = lambda outs, j: jnp.stack([o[j] for o in outs])
    return (xp, xs, stack(outs_p, 0), stack(outs_p, 1), stack(outs_p, 2), stack(outs_p, 3),
            stack(outs_s, 0), stack(outs_s, 1), stack(outs_s, 2), stack(outs_s, 3))
```

```python
import functools
import math

import jax
import jax.numpy as jnp
import numpy as np
from jax import lax
from jax.experimental import pallas as pl
from jax.experimental.pallas import tpu as pltpu

F32 = jnp.float32
BF16 = jnp.bfloat16

D_MODEL = 2048
MLA_HEADS = 8
QK_NOPE = 128
QK_ROPE = 64
QK_PAD = 256
V_HEAD = 128
KV_RANK = 512
ROPE_THETA = 10000.0
RW_HEADS = 16
RW_HEAD = 64
RW_WIDTH = RW_HEADS * RW_HEAD
DECAY_LORA = 64
AAA_LORA = 64
GATE_LORA = 160
RW_COLS = 3 * RW_WIDTH + DECAY_LORA + AAA_LORA + GATE_LORA
Q_COLS = MLA_HEADS * (QK_NOPE + QK_ROPE)
MLA_COLS = Q_COLS + KV_RANK + QK_ROPE
PEER_HEADS = 8
N_KEYS = 128
PK_DIM = 256
PK_HALF = PK_DIM // 2
PK_TOPK = 16
KEY_BITS = 7
TOPK_BITS = 4
PLE_DIM = 256
DEPTH = 1
ALPHA = (2 * DEPTH) ** 0.25
LN_EPS = 1e-5
RMS_EPS = 1e-6
GN_EPS = 64e-5
ATTN_SCALE = (QK_NOPE + QK_ROPE) ** -0.5

LANES = 128
SUBLANES = 8
RW_PAIRS = RW_HEADS // 2
DECODE_PAGES = 32
SCAN_DOT_PARTS = 2
PEER_TOKENS = 512
PEER_KEYS_PER_STEP = 8
SCAN_GROUP = SUBLANES
VMEM_LIMIT = 56 * 1024 * 1024

NT_DIMS = (((1,), (1,)), ((), ()))


def _cparams(sem, vmem=VMEM_LIMIT):
    return pltpu.CompilerParams(dimension_semantics=sem, vmem_limit_bytes=vmem)


def _resident(shape):
    nd = len(shape)
    return pl.BlockSpec(shape, lambda *_: (0,) * nd, pipeline_mode=pl.Buffered(1))


def _nt(a, b):
    return lax.dot_general(a, b, NT_DIMS, preferred_element_type=F32)


def _mm(a, b):
    return jnp.dot(a, b, preferred_element_type=F32)


def _dot01(x, e):
    h = x.astype(BF16)
    r = x - h.astype(F32)
    m = r.astype(BF16)
    l = (r - m.astype(F32)).astype(BF16)
    return _mm(h, e) + _mm(m, e) + _mm(l, e)


def _head_sum(x, e, et):
    return _dot01(_dot01(x, e), et)


def _sigmoid(x):
    return 1.0 / (1.0 + jnp.exp(-x))


def _layer_norm(x, g, b):
    mu = jnp.mean(x, axis=-1, keepdims=True)
    d = x - mu
    var = jnp.mean(d * d, axis=-1, keepdims=True)
    return d * lax.rsqrt(var + LN_EPS) * g + b


def _mla_proj_common(x_ref, w_ref, g_ref, cq_ref, sq_ref, ck_ref, sk_ref):
    x = x_ref[...].astype(BF16)
    res = _mm(x, w_ref[...])
    nq = MLA_HEADS * QK_NOPE
    rq = MLA_HEADS * QK_ROPE
    qn = res[:, :nq]
    qr = res[:, nq:nq + rq] * cq_ref[...] + res[:, nq + rq:nq + 2 * rq] * sq_ref[...]
    o = nq + 2 * rq
    c = res[:, o:o + KV_RANK]
    ckv = c * lax.rsqrt(jnp.mean(c * c, axis=-1, keepdims=True) + RMS_EPS) * g_ref[...]
    o += KV_RANK
    kr = res[:, o:o + QK_ROPE] * ck_ref[...] + res[:, o + QK_ROPE:o + 2 * QK_ROPE] * sk_ref[...]
    return qn, qr, ckv, kr


def _mla_proj_prompt_kernel(x_ref, w_ref, g_ref, cq_ref, sq_ref, ck_ref, sk_ref, wuk_ref, wuv_ref,
                            qc_o, kc_o, ckv_o, kr_o, v_o):
    qn, qr, ckv, kr = _mla_proj_common(x_ref, w_ref, g_ref, cq_ref, sq_ref, ck_ref, sk_ref)
    ckv_o[...] = ckv
    kr_o[...] = kr
    cb = ckv.astype(BF16)
    kn = _mm(cb, wuk_ref[...])
    v_o[...] = _mm(cb, wuv_ref[...]).astype(BF16)
    tm = qn.shape[0]
    low = lax.broadcasted_iota(jnp.int32, (tm, LANES), 1) < QK_ROPE
    kr_pad = jnp.concatenate([kr, jnp.zeros_like(kr)], axis=1).astype(BF16)
    for h in range(MLA_HEADS):
        blk = qr[:, (h // 2) * LANES:(h // 2 + 1) * LANES]
        if h % 2:
            blk = pltpu.roll(blk, QK_ROPE, axis=1)
        o = h * QK_PAD
        qc_o[:, o:o + QK_NOPE] = qn[:, h * QK_NOPE:(h + 1) * QK_NOPE].astype(BF16)
        qc_o[:, o + QK_NOPE:o + QK_PAD] = jnp.where(low, blk, 0.0).astype(BF16)
        kc_o[:, o:o + QK_NOPE] = kn[:, h * QK_NOPE:(h + 1) * QK_NOPE].astype(BF16)
        kc_o[:, o + QK_NOPE:o + QK_PAD] = kr_pad


def _mla_proj_sample_kernel(x_ref, w_ref, g_ref, cq_ref, sq_ref, ck_ref, sk_ref, wukt_ref,
                            ql_o, qr_o, ckv_o, kr_o):
    qn, qr, ckv, kr = _mla_proj_common(x_ref, w_ref, g_ref, cq_ref, sq_ref, ck_ref, sk_ref)
    qr_o[...] = qr.astype(BF16)
    ckv_o[...] = ckv
    kr_o[...] = kr
    for h in range(MLA_HEADS):
        qh = qn[:, h * QK_NOPE:(h + 1) * QK_NOPE].astype(BF16)
        ql_o[:, h * KV_RANK:(h + 1) * KV_RANK] = _mm(qh, wukt_ref[h]).astype(BF16)


def _mla_proj(x2d, w_all, kv_g, cq, sq, ck, sk, up_weights, prompt, tm):
    n = x2d.shape[0]
    wcols = w_all.shape[1]
    nq, rq = MLA_HEADS * QK_NOPE, MLA_HEADS * QK_ROPE
    row = lambda c: pl.BlockSpec((tm, c), lambda i: (i, 0))
    in_specs = [row(D_MODEL), _resident((D_MODEL, wcols)), _resident((1, KV_RANK)),
                row(rq), row(rq), row(QK_ROPE), row(QK_ROPE)]
    if prompt:
        kern = _mla_proj_prompt_kernel
        in_specs += [_resident((KV_RANK, nq)), _resident((KV_RANK, nq))]
        pad = MLA_HEADS * QK_PAD
        out_shape = [jax.ShapeDtypeStruct((n, pad), BF16), jax.ShapeDtypeStruct((n, pad), BF16),
                     jax.ShapeDtypeStruct((n, KV_RANK), F32), jax.ShapeDtypeStruct((n, QK_ROPE), F32),
                     jax.ShapeDtypeStruct((n, nq), BF16)]
        out_specs = [row(pad), row(pad), row(KV_RANK), row(QK_ROPE), row(nq)]
    else:
        kern = _mla_proj_sample_kernel
        in_specs += [_resident((MLA_HEADS, QK_NOPE, KV_RANK))]
        out_shape = [jax.ShapeDtypeStruct((n, MLA_HEADS * KV_RANK), BF16), jax.ShapeDtypeStruct((n, rq), BF16),
                     jax.ShapeDtypeStruct((n, KV_RANK), F32), jax.ShapeDtypeStruct((n, QK_ROPE), F32)]
        out_specs = [row(MLA_HEADS * KV_RANK), row(rq), row(KV_RANK), row(QK_ROPE)]
    return pl.pallas_call(
        kern, grid=(n // tm,), in_specs=in_specs, out_specs=out_specs, out_shape=out_shape,
        compiler_params=_cparams(("parallel",)), name="mla_proj",
    )(x2d, w_all, kv_g, cq, sq, ck, sk, *up_weights)


def _flash_kernel(q_ref, k_ref, v_ref, o_ref, m_ref, l_ref, acc_ref, s_ref, p_ref, *, tq, tk, nk):
    i = pl.program_id(1)
    j = pl.program_id(2)

    @pl.when(j == 0)
    def _():
        m_ref[...] = jnp.full(m_ref.shape, -jnp.inf, F32)
        l_ref[...] = jnp.zeros(l_ref.shape, F32)
        acc_ref[...] = jnp.zeros(acc_ref.shape, F32)

    def sweep(masked):
        if masked:
            qpos = i * tq + lax.broadcasted_iota(jnp.int32, (tq, tk), 0)
            kpos = j * tk + lax.broadcasted_iota(jnp.int32, (tq, tk), 1)
            visible = kpos <= qpos
        for h in range(MLA_HEADS):
            hs = slice(h * QK_PAD, (h + 1) * QK_PAD)
            s = _nt(q_ref[0, :, hs], k_ref[0, :, hs]) * ATTN_SCALE
            s_ref[h] = jnp.where(visible, s, -jnp.inf) if masked else s
        alphas = []
        for h in range(MLA_HEADS):
            s = s_ref[h]
            m_prev = m_ref[h]
            m_new = jnp.maximum(m_prev, jnp.max(s, axis=1, keepdims=True))
            alpha = jnp.exp(m_prev - m_new)
            p = jnp.exp(s - m_new)
            l_ref[h] = alpha * l_ref[h] + jnp.sum(p, axis=1, keepdims=True)
            m_ref[h] = m_new
            p_ref[h] = p.astype(BF16)
            alphas.append(alpha)
        for h in range(MLA_HEADS):
            acc_ref[h] = alphas[h] * acc_ref[h] + _mm(p_ref[h], v_ref[0, :, h * V_HEAD:(h + 1) * V_HEAD])

    first_q, last_q = i * tq, i * tq + tq - 1
    first_k, last_k = j * tk, j * tk + tk - 1

    @pl.when(last_k <= first_q)
    def _():
        sweep(False)

    @pl.when((last_k > first_q) & (first_k <= last_q))
    def _():
        sweep(True)

    @pl.when(j == nk - 1)
    def _():
        for h in range(MLA_HEADS):
            o_ref[0, :, h * V_HEAD:(h + 1) * V_HEAD] = (acc_ref[h] / l_ref[h]).astype(o_ref.dtype)


def _mla_prompt_attention(q, k, v, tq, tk):
    b, s, _ = q.shape
    nq, nk = s // tq, s // tk
    last = lambda i: (i * tq + tq - 1) // tk
    qspec = lambda c: pl.BlockSpec((1, tq, c), lambda bb, i, j: (bb, i, 0))
    kspec = lambda c: pl.BlockSpec((1, tk, c), lambda bb, i, j: (bb, jnp.minimum(j, last(i)), 0))
    return pl.pallas_call(
        functools.partial(_flash_kernel, tq=tq, tk=tk, nk=nk),
        grid=(b, nq, nk),
        in_specs=[qspec(q.shape[2]), kspec(k.shape[2]), kspec(v.shape[2])],
        out_specs=qspec(v.shape[2]),
        out_shape=jax.ShapeDtypeStruct(v.shape, BF16),
        scratch_shapes=[pltpu.VMEM((MLA_HEADS, tq, 1), F32), pltpu.VMEM((MLA_HEADS, tq, 1), F32),
                        pltpu.VMEM((MLA_HEADS, tq, V_HEAD), F32),
                        pltpu.VMEM((MLA_HEADS, tq, tk), F32), pltpu.VMEM((MLA_HEADS, tq, tk), BF16)],
        compiler_params=_cparams(("parallel", "parallel", "arbitrary")), name="mla_prompt_attention",
    )(q, k, v)


def _decode_kernel(pt_ref, ql_ref, qr_ref, cn_ref, kn_ref, ckv_hbm, krt_hbm, o_ref,
                   m_ref, l_ref, acc_ref, cbuf, kbuf, csem, ksem, *, pages, ng, n_steps, before_wait=None):
    step = pl.program_id(0) * ng + pl.program_id(1)
    slot = step % 2

    def copies(s, sl):
        out = []
        for j in range(pages):
            pg = pt_ref[s * pages + j]
            out.append(pltpu.make_async_copy(ckv_hbm.at[pg], cbuf.at[sl, j], csem.at[sl]))
            out.append(pltpu.make_async_copy(krt_hbm.at[pg], kbuf.at[sl, j], ksem.at[sl]))
        return out

    @pl.when(step == 0)
    def _():
        for cp in copies(step, slot):
            cp.start()

    @pl.when(step + 1 < n_steps)
    def _():
        for cp in copies(step + 1, 1 - slot):
            cp.start()

    if before_wait is not None:
        before_wait()
    for cp in copies(step, slot):
        cp.wait()
    _decode_init(m_ref, l_ref, acc_ref)
    _decode_sweep(ql_ref, qr_ref, [cbuf.at[slot, j] for j in range(pages)],
                  [kbuf.at[slot, j] for j in range(pages)], m_ref, l_ref, acc_ref)
    _decode_final(ql_ref, qr_ref, cn_ref, kn_ref, o_ref, m_ref, l_ref, acc_ref, ng)


def _decode_init(m_ref, l_ref, acc_ref):
    @pl.when(pl.program_id(1) == 0)
    def _():
        m_ref[...] = jnp.full(m_ref.shape, -jnp.inf, F32)
        l_ref[...] = jnp.zeros(l_ref.shape, F32)
        acc_ref[...] = jnp.zeros(acc_ref.shape, F32)


def _decode_sweep(ql_ref, qr_ref, c_refs, k_refs, m_ref, l_ref, acc_ref):
    pages = len(c_refs)
    ql = ql_ref[0]
    qr = qr_ref[0]
    cs, ss = [], []
    for i in range(pages):
        c = c_refs[i][...].astype(BF16)
        k = k_refs[i][...].astype(BF16)
        cs.append(c)
        ss.append(_nt(ql, c) + _mm(qr, k))
    s = jnp.concatenate(ss, axis=1) * ATTN_SCALE
    m_prev = m_ref[...]
    m_new = jnp.maximum(m_prev, jnp.max(s, axis=1, keepdims=True))
    alpha = jnp.exp(m_prev - m_new)
    p = jnp.exp(s - m_new)
    pb = p.astype(BF16)
    page = cs[0].shape[0]
    pv = _mm(pb[:, :page], cs[0])
    for i in range(1, pages):
        pv = pv + _mm(pb[:, i * page:(i + 1) * page], cs[i])
    m_ref[...] = m_new
    l_ref[...] = alpha * l_ref[...] + jnp.sum(p, axis=1, keepdims=True)
    acc_ref[...] = alpha * acc_ref[...] + pv


def _decode_final(ql_ref, qr_ref, cn_ref, kn_ref, o_ref, m_ref, l_ref, acc_ref, ng):
    @pl.when(pl.program_id(1) == ng - 1)
    def _():
        ql, qr = ql_ref[0], qr_ref[0]
        m_new = m_ref[...]
        cn = cn_ref[0].astype(BF16).astype(F32)
        kn = kn_ref[0].astype(BF16).astype(F32)
        s_self = (jnp.sum(ql.astype(F32) * cn, axis=1, keepdims=True)
                  + jnp.sum(qr.astype(F32) * kn, axis=1, keepdims=True)) * ATTN_SCALE
        m_f = jnp.maximum(m_new, s_self)
        a_f = jnp.exp(m_new - m_f)
        p_self = jnp.exp(s_self - m_f)
        l_f = a_f * l_ref[...] + p_self
        acc_f = a_f * acc_ref[...] + p_self.astype(BF16).astype(F32) * cn
        o_ref[0] = acc_f / l_f


def _mla_sample_attention(page_table, q_lat, q_rope, c_new, kr_new, cache_ckv, cache_krope_t, pages):
    db, n_pages = page_table.shape
    _, page, _ = cache_ckv.shape
    ng = n_pages // pages
    pt = page_table.reshape(-1)
    per_seq = lambda shape: pl.BlockSpec((1,) + shape, lambda b, g, pt_: (b, 0, 0))
    in_hbm = pl.BlockSpec(memory_space=pl.ANY)
    grid_spec = pltpu.PrefetchScalarGridSpec(
        num_scalar_prefetch=1, grid=(db, ng),
        in_specs=[per_seq((MLA_HEADS, KV_RANK)), per_seq((MLA_HEADS, QK_ROPE)),
                  per_seq((1, KV_RANK)), per_seq((1, QK_ROPE)), in_hbm, in_hbm],
        out_specs=per_seq((MLA_HEADS, KV_RANK)),
        scratch_shapes=[pltpu.VMEM((MLA_HEADS, 1), F32), pltpu.VMEM((MLA_HEADS, 1), F32),
                        pltpu.VMEM((MLA_HEADS, KV_RANK), F32),
                        pltpu.VMEM((2, pages, page, KV_RANK), F32), pltpu.VMEM((2, pages, QK_ROPE, page), F32),
                        pltpu.SemaphoreType.DMA((2,)), pltpu.SemaphoreType.DMA((2,))])
    return pl.pallas_call(
        functools.partial(_decode_kernel, pages=pages, ng=ng, n_steps=db * ng),
        grid_spec=grid_spec,
        out_shape=jax.ShapeDtypeStruct((db, MLA_HEADS, KV_RANK), F32),
        compiler_params=_cparams(("arbitrary", "arbitrary")), name="mla_sample_attention",
    )(pt, q_lat, q_rope, c_new, kr_new, cache_ckv, cache_krope_t)


def _head_up_kernel(o_ref, w_ref, out_ref):
    out_ref[...] = _mm(o_ref[...].astype(BF16), w_ref[...]).astype(out_ref.dtype)


def _head_up(o_lat2d, w_uv2):
    n = o_lat2d.shape[0]
    return pl.pallas_call(
        _head_up_kernel, grid=(MLA_HEADS,),
        in_specs=[pl.BlockSpec((n, KV_RANK), lambda h: (0, h)), pl.BlockSpec((KV_RANK, V_HEAD), lambda h: (0, h))],
        out_specs=pl.BlockSpec((n, V_HEAD), lambda h: (0, h)),
        out_shape=jax.ShapeDtypeStruct((n, MLA_HEADS * V_HEAD), BF16),
        compiler_params=_cparams(("parallel",)), name="mla_head_up",
    )(o_lat2d, w_uv2)


def _rwkv_terms(p, prev, mu_ref, w0_ref, wup_ref, a0_ref, aup_ref, gup_ref, kk_ref, ka_ref, rk_ref,
                e_ref, et_ref, outs):
    r_o, w_o, k_o, v_o, n_o, b_o, g_o, bonus_o = outs
    xs = p + mu_ref[...] * (prev - p)
    w3 = 3 * RW_WIDTH
    r = xs[:, :RW_WIDTH]
    k0 = xs[:, RW_WIDTH:2 * RW_WIDTH]
    v = xs[:, 2 * RW_WIDTH:w3]
    xw = xs[:, w3:w3 + DECAY_LORA]
    xa = xs[:, w3 + DECAY_LORA:w3 + DECAY_LORA + AAA_LORA]
    xg = xs[:, w3 + DECAY_LORA + AAA_LORA:]
    y = w0_ref[...] + _mm(jnp.tanh(xw).astype(BF16), wup_ref[...])
    decay = jnp.exp(-math.exp(-0.5) * _sigmoid(y))
    a = _sigmoid(a0_ref[...] + _mm(xa.astype(BF16), aup_ref[...]))
    g = _mm(_sigmoid(xg).astype(BF16), gup_ref[...])
    kk = k0 * kk_ref[...]
    e, et = e_ref[...], et_ref[...]
    kk = kk * lax.rsqrt(jnp.maximum(_head_sum(kk * kk, e, et), 1e-24))
    k = k0 * (1.0 + (a - 1.0) * ka_ref[...])
    r_o[...] = r
    w_o[...] = decay
    k_o[...] = k
    v_o[...] = v
    n_o[...] = -kk
    b_o[...] = kk * a
    g_o[...] = g
    bonus_o[...] = _head_sum(r * k * rk_ref[...], e, et) * v


def _rwkv_proj_prompt_kernel(x_ref, w_ref, mu_ref, w0_ref, wup_ref, a0_ref, aup_ref, gup_ref,
                             kk_ref, ka_ref, rk_ref, e_ref, et_ref,
                             r_o, w_o, k_o, v_o, n_o, b_o, g_o, bonus_o, last_o, carry_ref):
    tm = x_ref.shape[1]

    @pl.when(pl.program_id(1) == 0)
    def _():
        carry_ref[...] = jnp.zeros(carry_ref.shape, F32)

    p = _mm(x_ref[0].astype(BF16), w_ref[...])
    rows = lax.broadcasted_iota(jnp.int32, p.shape, 0)
    prev = jnp.where(rows == 0, carry_ref[...], pltpu.roll(p, 1, axis=0))
    last = p[tm - 1:tm, :]
    carry_ref[...] = last
    last_o[0] = last
    outs = tuple(o.at[0] for o in (r_o, w_o, k_o, v_o, n_o, b_o, g_o, bonus_o))
    _rwkv_terms(p, prev, mu_ref, w0_ref, wup_ref, a0_ref, aup_ref, gup_ref, kk_ref, ka_ref, rk_ref,
                e_ref, et_ref, outs)


def _rwkv_proj_sample_kernel(x_ref, prev_ref, w_ref, mu_ref, w0_ref, wup_ref, a0_ref, aup_ref, gup_ref,
                             kk_ref, ka_ref, rk_ref, e_ref, et_ref,
                             r_o, w_o, k_o, v_o, n_o, b_o, g_o, bonus_o, last_o):
    p = _mm(x_ref[...].astype(BF16), w_ref[...])
    last_o[...] = p
    _rwkv_terms(p, prev_ref[...], mu_ref, w0_ref, wup_ref, a0_ref, aup_ref, gup_ref, kk_ref, ka_ref, rk_ref,
                e_ref, et_ref, (r_o, w_o, k_o, v_o, n_o, b_o, g_o, bonus_o))


def _rwkv_weight_specs():
    vec = lambda c: _resident((1, c))
    return [_resident((D_MODEL, RW_COLS)), vec(RW_COLS), vec(RW_WIDTH), _resident((DECAY_LORA, RW_WIDTH)),
            vec(RW_WIDTH), _resident((AAA_LORA, RW_WIDTH)), _resident((GATE_LORA, RW_WIDTH)),
            vec(RW_WIDTH), vec(RW_WIDTH), vec(RW_WIDTH),
            _resident((RW_WIDTH, LANES)), _resident((LANES, RW_WIDTH))]


def _rwkv_proj_prompt(x, rw_weights, tm):
    b, t, _ = x.shape
    tile = lambda c: pl.BlockSpec((1, tm, c), lambda bb, i: (bb, i, 0))
    wide = jax.ShapeDtypeStruct((b, t, RW_WIDTH), F32)
    return pl.pallas_call(
        _rwkv_proj_prompt_kernel, grid=(b, t // tm),
        in_specs=[tile(D_MODEL)] + _rwkv_weight_specs(),
        out_specs=[tile(RW_WIDTH)] * 8 + [pl.BlockSpec((1, 1, RW_COLS), lambda bb, i: (bb, 0, 0))],
        out_shape=[wide] * 8 + [jax.ShapeDtypeStruct((b, 1, RW_COLS), F32)],
        scratch_shapes=[pltpu.VMEM((1, RW_COLS), F32)],
        compiler_params=_cparams(("parallel", "arbitrary")), name="rwkv_proj_prompt",
    )(x, *rw_weights)


def _rwkv_proj_sample(x2d, prev, rw_weights):
    n = x2d.shape[0]
    full = lambda c: pl.BlockSpec((n, c), lambda i: (0, 0))
    wide = jax.ShapeDtypeStruct((n, RW_WIDTH), F32)
    return pl.pallas_call(
        _rwkv_proj_sample_kernel, grid=(1,),
        in_specs=[full(D_MODEL), full(RW_COLS)] + _rwkv_weight_specs(),
        out_specs=[full(RW_WIDTH)] * 8 + [full(RW_COLS)],
        out_shape=[wide] * 8 + [jax.ShapeDtypeStruct((n, RW_COLS), F32)],
        compiler_params=_cparams(("arbitrary",)), name="rwkv_proj_sample",
    )(x2d, prev, *rw_weights)


def _scan_load_state(s0_ref, state_ref):
    for b in range(state_ref.shape[0]):
        for p in range(RW_PAIRS):
            state_ref[b, p] = jnp.concatenate([s0_ref[b, 2 * p], s0_ref[b, 2 * p + 1]], axis=1)


def _scan_store_state(st_ref, state_ref):
    for b in range(state_ref.shape[0]):
        for p in range(RW_PAIRS):
            s = state_ref[b, p]
            st_ref[b, 2 * p] = s[:, :RW_HEAD]
            st_ref[b, 2 * p + 1] = s[:, RW_HEAD:]


def _scan_steps(ins, e2_ref, eo_ref, vt_ref, ev_ref, out_ref, state_ref, vcol_ref, group):
    nb = state_ref.shape[0]
    chains = [(b, p) for b in range(nb) for p in range(RW_PAIRS)]
    lane = lax.broadcasted_iota(jnp.int32, (RW_HEAD, LANES), 1)
    sub = lax.broadcasted_iota(jnp.int32, (RW_HEAD, LANES), 0)
    diag = (lane & (RW_HEAD - 1)) == sub
    e2 = e2_ref[...]

    def pieces(x):
        hi = x.astype(BF16)
        return jnp.concatenate([hi, (x - hi.astype(F32)).astype(BF16)], axis=1)

    def head_sums(xs):
        out = []
        per = max(len(xs) // SCAN_DOT_PARTS, 1)
        for q in range(0, len(xs), per):
            res = _mm(jnp.concatenate([pieces(x) for x in xs[q:q + per]], axis=0), e2)
            out += [res[i * RW_HEAD:(i + 1) * RW_HEAD] for i in range(per)]
        return out

    def head_sums_out(xs):
        eo = eo_ref[...]
        out = []
        per = max(len(xs) // SCAN_DOT_PARTS, 2)
        for q in range(0, len(xs), per):
            res = _mm(jnp.concatenate([jnp.concatenate([xs[i].astype(BF16), xs[i + 1].astype(BF16)], axis=1)
                                       for i in range(q, q + per, 2)], axis=0), eo)
            for i in range(per // 2):
                blk = res[i * RW_HEAD:(i + 1) * RW_HEAD]
                out += [blk[:, :LANES], blk[:, LANES:]]
        return out

    def steps(tg):
        t0 = tg * group if isinstance(tg, int) else pl.multiple_of(tg * group, group)
        rows = [[ref[b, pl.ds(t0, group), p * LANES:(p + 1) * LANES] for ref in ins] for b, p in chains]
        outs = [[] for _ in chains]
        if group > 1:
            lhs = [pieces(jnp.where((lane % RW_HEADS) // 2 == p, vt_ref[b, tg], 0.0)) for b, p in chains]
            vcol_ref[...] = _mm(jnp.concatenate(lhs, axis=0), ev_ref[...])
        for i in range(group):
            r, w, k, v, n, bb = ([x[j][i:i + 1, :] for x in rows] for j in range(len(ins)))
            if group > 1:
                v_cols = [vcol_ref[ci * RW_HEAD:(ci + 1) * RW_HEAD, i * LANES:(i + 1) * LANES]
                          for ci in range(len(chains))]
            else:
                v_cols = head_sums([jnp.where(diag, vi, 0.0) for vi in v])
            sa = head_sums([state_ref[b, p] * n[ci] for ci, (b, p) in enumerate(chains)])
            new = []
            for ci, (b, p) in enumerate(chains):
                s = state_ref[b, p] * w[ci] + sa[ci] * bb[ci] + v_cols[ci] * k[ci]
                state_ref[b, p] = s
                new.append(s * r[ci])
            for ci, o_col in enumerate(head_sums_out(new)):
                outs[ci].append(jnp.sum(jnp.where(diag, o_col, 0.0), axis=0, keepdims=True))
        for ci, (b, p) in enumerate(chains):
            out_ref[b, pl.ds(t0, group), p * LANES:(p + 1) * LANES] = (
                outs[ci][0] if group == 1 else jnp.concatenate(outs[ci], axis=0))

    return steps


def _scan_loop(steps, n_groups):
    if n_groups == 1:
        steps(0)
    else:
        def body(tg, carry):
            steps(tg)
            return carry
        lax.fori_loop(0, n_groups, body, 0)


def _rwkv_scan_kernel(s0_ref, r_ref, w_ref, k_ref, v_ref, n_ref, b_ref, e2_ref, eo_ref, *rest, tc, nc):
    c = pl.program_id(1)
    group = SCAN_GROUP if tc % SCAN_GROUP == 0 else 1
    if group > 1:
        vt_ref, ev_ref, out_ref, st_ref, state_ref, vcol_ref = rest
    else:
        (out_ref, st_ref, state_ref), vt_ref, ev_ref, vcol_ref = rest, None, None, None

    @pl.when(c == 0)
    def _():
        _scan_load_state(s0_ref, state_ref)

    _scan_loop(_scan_steps((r_ref, w_ref, k_ref, v_ref, n_ref, b_ref), e2_ref, eo_ref, vt_ref, ev_ref,
                           out_ref, state_ref, vcol_ref, group), tc // group)

    @pl.when(c == nc - 1)
    def _():
        _scan_store_state(st_ref, state_ref)


def _scan_constants(e2):
    same_head = e2[:LANES]
    zero = jnp.zeros_like(same_head)
    eo = jnp.concatenate([jnp.concatenate([same_head, zero], axis=1),
                          jnp.concatenate([zero, same_head], axis=1)], axis=0)
    src = jnp.arange(2 * LANES) % LANES
    dst = jnp.arange(SCAN_GROUP * LANES)
    ev = ((src[:, None] // RW_HEADS == dst[None, :] // LANES)
          & (src[:, None] % 2 == (dst[None, :] % LANES) // RW_HEAD))
    return eo, ev.astype(BF16)


def _transposed_v(v):
    b, t, _ = v.shape
    return jnp.swapaxes(v.reshape(b, t // SCAN_GROUP, SCAN_GROUP * RW_HEADS, RW_HEAD), 2, 3)


def _rwkv_scan(s0, r, w, k, v, n, bb, e2, nb, tc):
    b, t, _ = r.shape
    nc = t // tc
    seq = pl.BlockSpec((nb, tc, RW_WIDTH), lambda i, c: (i, c, 0))
    st = pl.BlockSpec((nb, RW_HEADS, RW_HEAD, RW_HEAD), lambda i, c: (i, 0, 0, 0))
    eo, ev = _scan_constants(e2)
    in_specs = [st] + [seq] * 6 + [_resident(e2.shape), _resident(eo.shape)]
    args = [s0, r, w, k, v, n, bb, e2, eo]
    scratch = [pltpu.VMEM((nb, RW_PAIRS, RW_HEAD, LANES), F32)]
    if tc % SCAN_GROUP == 0:
        in_specs += [pl.BlockSpec((nb, tc // SCAN_GROUP, RW_HEAD, LANES), lambda i, c: (i, c, 0, 0)),
                     _resident(ev.shape)]
        args += [_transposed_v(v), ev]
        scratch.append(pltpu.VMEM((nb * RW_PAIRS * RW_HEAD, SCAN_GROUP * LANES), F32))
    return pl.pallas_call(
        functools.partial(_rwkv_scan_kernel, tc=tc, nc=nc),
        grid=(b // nb, nc),
        in_specs=in_specs,
        out_specs=[seq, st],
        out_shape=[jax.ShapeDtypeStruct((b, t, RW_WIDTH), F32),
                   jax.ShapeDtypeStruct((b, RW_HEADS, RW_HEAD, RW_HEAD), F32)],
        scratch_shapes=scratch,
        compiler_params=_cparams(("parallel", "arbitrary")), name="rwkv_scan",
    )(*args)


def _decode_scan_kernel(pt_ref, ql_ref, qr_ref, cn_ref, kn_ref, ckv_hbm, krt_hbm,
                        s0_ref, r_ref, w_ref, k_ref, v_ref, n_ref, b_ref, e2_ref, eo_ref, vt_ref, ev_ref,
                        o_ref, out_ref, st_ref, m_ref, l_ref, acc_ref, cbuf, kbuf, csem, ksem, state_ref, vcol_ref,
                        *, pages, ng, n_steps):
    step = pl.program_id(0) * ng + pl.program_id(1)

    @pl.when(step == 0)
    def _():
        _scan_load_state(s0_ref, state_ref)

    steps = _scan_steps((r_ref, w_ref, k_ref, v_ref, n_ref, b_ref), e2_ref, eo_ref, vt_ref, ev_ref,
                        out_ref, state_ref, vcol_ref, SCAN_GROUP)
    _decode_kernel(pt_ref, ql_ref, qr_ref, cn_ref, kn_ref, ckv_hbm, krt_hbm, o_ref, m_ref, l_ref, acc_ref,
                   cbuf, kbuf, csem, ksem, pages=pages, ng=ng, n_steps=n_steps, before_wait=lambda: steps(0))

    @pl.when(step == n_steps - 1)
    def _():
        _scan_store_state(st_ref, state_ref)


def _decode_and_scan(page_table, q_lat, q_rope, c_new, kr_new, cache_ckv, cache_krope_t, pages,
                     s0, r, w, k, v, n, bb, e2):
    db, n_pages = page_table.shape
    _, page, _ = cache_ckv.shape
    ng = n_pages // pages
    b, t, _ = r.shape
    assert t == SCAN_GROUP * db * ng
    eo, ev = _scan_constants(e2)
    per_seq = lambda shape: pl.BlockSpec((1,) + shape, lambda i, g, pt_: (i, 0, 0))
    in_hbm = pl.BlockSpec(memory_space=pl.ANY)
    seq = pl.BlockSpec((b, SCAN_GROUP, RW_WIDTH), lambda i, g, pt_: (0, i * ng + g, 0))
    st = pl.BlockSpec((b, RW_HEADS, RW_HEAD, RW_HEAD), lambda i, g, pt_: (0, 0, 0, 0))
    const = lambda a: pl.BlockSpec(a.shape, lambda i, g, pt_: (0,) * a.ndim, pipeline_mode=pl.Buffered(1))
    grid_spec = pltpu.PrefetchScalarGridSpec(
        num_scalar_prefetch=1, grid=(db, ng),
        in_specs=[per_seq((MLA_HEADS, KV_RANK)), per_seq((MLA_HEADS, QK_ROPE)),
                  per_seq((1, KV_RANK)), per_seq((1, QK_ROPE)), in_hbm, in_hbm,
                  st] + [seq] * 6 + [const(e2), const(eo),
                  pl.BlockSpec((b, 1, RW_HEAD, LANES), lambda i, g, pt_: (0, i * ng + g, 0, 0)), const(ev)],
        out_specs=[per_seq((MLA_HEADS, KV_RANK)), seq, st],
        scratch_shapes=[pltpu.VMEM((MLA_HEADS, 1), F32), pltpu.VMEM((MLA_HEADS, 1), F32),
                        pltpu.VMEM((MLA_HEADS, KV_RANK), F32),
                        pltpu.VMEM((2, pages, page, KV_RANK), F32), pltpu.VMEM((2, pages, QK_ROPE, page), F32),
                        pltpu.SemaphoreType.DMA((2,)), pltpu.SemaphoreType.DMA((2,)),
                        pltpu.VMEM((b, RW_PAIRS, RW_HEAD, LANES), F32),
                        pltpu.VMEM((b * RW_PAIRS * RW_HEAD, SCAN_GROUP * LANES), F32)])
    return pl.pallas_call(
        functools.partial(_decode_scan_kernel, pages=pages, ng=ng, n_steps=db * ng),
        grid_spec=grid_spec,
        out_shape=[jax.ShapeDtypeStruct((db, MLA_HEADS, KV_RANK), F32),
                   jax.ShapeDtypeStruct((b, t, RW_WIDTH), F32),
                   jax.ShapeDtypeStruct((b, RW_HEADS, RW_HEAD, RW_HEAD), F32)],
        compiler_params=_cparams(("arbitrary", "arbitrary")), name="mla_sample_attention_rwkv_scan",
    )(page_table.reshape(-1), q_lat, q_rope, c_new, kr_new, cache_ckv, cache_krope_t,
      s0, r, w, k, v, n, bb, e2, eo, _transposed_v(v), ev)


def _mix_out_kernel(x_ref, om_ref, raw_ref, g_ref, bonus_ref, gng_ref, gnb_ref, e_ref, et_ref,
                    wo1_ref, wo2_ref, lng_ref, lnb_ref, o_ref, ob_ref):
    e, et = e_ref[...], et_ref[...]
    raw = raw_ref[...]
    mu = _head_sum(raw, e, et) * (1.0 / RW_HEAD)
    d = raw - mu
    var = _head_sum(d * d, e, et) * (1.0 / RW_HEAD)
    o_rw = (d * lax.rsqrt(var + GN_EPS) * gng_ref[...] + gnb_ref[...] + bonus_ref[...]) * g_ref[...]
    mixed = _mm(om_ref[...], wo1_ref[...]) + _mm(o_rw.astype(BF16), wo2_ref[...])
    x1 = _layer_norm(ALPHA * x_ref[...] + mixed, lng_ref[...], lnb_ref[...])
    o_ref[...] = x1
    ob_ref[...] = x1.astype(BF16)


def _mix_out(x2d, o_mla, raw, g, bonus, gng, gnb, e, et, wo1, wo2, lng, lnb, tm):
    n = x2d.shape[0]
    row = lambda c: pl.BlockSpec((tm, c), lambda i: (i, 0))
    vec = lambda c: _resident((1, c))
    return pl.pallas_call(
        _mix_out_kernel, grid=(n // tm,),
        in_specs=[row(D_MODEL), row(RW_WIDTH), row(RW_WIDTH), row(RW_WIDTH), row(RW_WIDTH),
                  vec(RW_WIDTH), vec(RW_WIDTH), _resident((RW_WIDTH, LANES)), _resident((LANES, RW_WIDTH)),
                  _resident(wo1.shape), _resident(wo2.shape), vec(D_MODEL), vec(D_MODEL)],
        out_specs=[row(D_MODEL), row(D_MODEL)],
        out_shape=[jax.ShapeDtypeStruct((n, D_MODEL), F32), jax.ShapeDtypeStruct((n, D_MODEL), BF16)],
        compiler_params=_cparams(("parallel",)), name="mix_out_ln1",
    )(x2d, o_mla, raw, g, bonus, gng, gnb, e, et, wo1, wo2, lng, lnb)


def _peer_route_kernel(x_ref, wq_ref, k1_ref, k2_ref, pa_ref, pb_ref, flat_ref, i1_o, i2_o, g_o):
    tm = x_ref.shape[0]
    kk2 = PK_TOPK * PK_TOPK
    q = _mm(x_ref[...].astype(BF16), wq_ref[...])
    key_row = lax.broadcasted_iota(jnp.int32, (N_KEYS, tm), 0).astype(F32)
    rank_row = lax.broadcasted_iota(jnp.int32, (PK_TOPK, tm), 0)
    neg = -jnp.inf
    pa, pb = pa_ref[...], pb_ref[...]
    flat = jnp.concatenate([flat_ref[...]] * -(-tm // LANES), axis=1)[:, :tm]
    is_pair = flat < float(kk2)

    def pick(p, x):
        h = x.astype(BF16)
        r = x - h.astype(F32)
        m = r.astype(BF16)
        l = (r - m.astype(F32)).astype(BF16)
        return _mm(p, h) + _mm(p, m) + _mm(p, l)

    def top_keys(scores):
        def body(r, carry):
            hit = rank_row == r
            ms = [jnp.max(s, axis=0, keepdims=True) for s, _, _ in carry]
            ids = [jnp.min(jnp.where(s == m, key_row, float(N_KEYS)), axis=0, keepdims=True)
                   for (s, _, _), m in zip(carry, ms)]
            return tuple((jnp.where(key_row == idx, neg, s), jnp.where(hit, m, vals), jnp.where(hit, idx, idxs))
                         for (s, vals, idxs), m, idx in zip(carry, ms, ids))
        zero = jnp.zeros((PK_TOPK, tm), F32)
        res = lax.fori_loop(0, PK_TOPK, body, tuple((s, zero, zero) for s in scores))
        return [(vals, idxs) for _, vals, idxs in res]

    experts, gates = [], []
    for h in range(PEER_HEADS):
        q1 = q[:, h * PK_DIM:h * PK_DIM + PK_HALF].astype(BF16)
        q2 = q[:, h * PK_DIM + PK_HALF:(h + 1) * PK_DIM].astype(BF16)
        (v1, i1), (v2, i2) = top_keys([_nt(k1_ref[h], q1), _nt(k2_ref[h], q2)])
        cand = jnp.where(is_pair, pick(pa, v1) + pick(pb, v2), neg)
        expert = pick(pa, i1) * float(N_KEYS) + pick(pb, i2)

        def body(r, carry, expert=expert):
            cnd, top, den, e_a, p_a = carry
            m = jnp.max(cnd, axis=0, keepdims=True)
            idx = jnp.min(jnp.where(cnd == m, flat, float(kk2)), axis=0, keepdims=True)
            hit = flat == idx
            e = jnp.max(jnp.where(hit, expert, -1.0), axis=0, keepdims=True)
            cnd = jnp.where(hit, neg, cnd)
            top = jnp.where(r == 0, m, top)
            pr = jnp.exp(m - top)
            sel = rank_row == r
            return cnd, top, den + pr, jnp.where(sel, e, e_a), jnp.where(sel, pr, p_a)

        zrow = jnp.zeros((1, tm), F32)
        zk = jnp.zeros((PK_TOPK, tm), F32)
        _, _, den, e_h, p_h = lax.fori_loop(0, PK_TOPK, body, (cand, zrow, zrow, zk, zk))
        experts.append(e_h)
        gates.append(p_h / den)
    e_int = jnp.concatenate(experts, axis=0).T.astype(jnp.int32)
    i1_o[...] = e_int >> KEY_BITS
    i2_o[...] = e_int & (N_KEYS - 1)
    g_o[...] = jnp.concatenate(gates, axis=0).T


def _candidate_pairs():
    pairs = [(a, b) for a in range(PK_TOPK) for b in range(PK_TOPK // (a + 1))]
    rows = -(-len(pairs) // 16) * 16
    pa = np.zeros((rows, PK_TOPK), np.float32)
    pb = np.zeros((rows, PK_TOPK), np.float32)
    flat = np.full((rows, LANES), float(PK_TOPK * PK_TOPK), np.float32)
    for i, (a, b) in enumerate(pairs):
        pa[i, a] = 1.0
        pb[i, b] = 1.0
        flat[i, :] = a * PK_TOPK + b
    return jnp.asarray(pa, BF16), jnp.asarray(pb, BF16), jnp.asarray(flat)


def _peer_route(x2d, wq, k1, k2, tm):
    n = x2d.shape[0]
    slots = PEER_HEADS * PK_TOPK
    row = lambda c: pl.BlockSpec((tm, c), lambda i: (i, 0))
    pa, pb, flat = _candidate_pairs()
    return pl.pallas_call(
        _peer_route_kernel, grid=(n // tm,),
        in_specs=[row(D_MODEL), _resident(wq.shape), _resident(k1.shape), _resident(k2.shape),
                  _resident(pa.shape), _resident(pb.shape), _resident(flat.shape)],
        out_specs=[row(slots)] * 3,
        out_shape=[jax.ShapeDtypeStruct((n, slots), jnp.int32), jax.ShapeDtypeStruct((n, slots), jnp.int32),
                   jax.ShapeDtypeStruct((n, slots), F32)],
        compiler_params=_cparams(("parallel",)), name="peer_route",
    )(x2d, wq, k1, k2, pa, pb, flat)


def _peer_weights_kernel(i1_ref, i2_ref, g_ref, wt_ref):
    tm = i1_ref.shape[0]
    key = lax.broadcasted_iota(jnp.int32, (N_KEYS, PEER_HEADS * PK_TOPK), 0)

    def body(grp, carry):
        t0 = pl.multiple_of(grp * SUBLANES, SUBLANES)
        i1s = i1_ref[pl.ds(t0, SUBLANES), :]
        i2s = i2_ref[pl.ds(t0, SUBLANES), :]
        gs = g_ref[pl.ds(t0, SUBLANES), :]
        per_token = []
        for j in range(SUBLANES):
            g1 = jnp.where(key == i1s[j:j + 1, :], gs[j:j + 1, :], 0.0)
            hi = g1.astype(BF16)
            lo = (g1 - hi.astype(F32)).astype(BF16)
            g2 = jnp.where(key == i2s[j:j + 1, :], 1.0, 0.0).astype(BF16)
            per_token.append(_nt(jnp.concatenate([hi, lo], axis=1), jnp.concatenate([g2, g2], axis=1)))
        wt_ref[grp] = jnp.swapaxes(jnp.stack(per_token, axis=0), 0, 1)
        return carry

    lax.fori_loop(0, tm // SUBLANES, body, 0)


def _peer_weights(i1, i2, g, tm):
    n, slots = i1.shape
    row = pl.BlockSpec((tm, slots), lambda i: (i, 0))
    return pl.pallas_call(
        _peer_weights_kernel, grid=(n // tm,),
        in_specs=[row, row, row],
        out_specs=pl.BlockSpec((tm // SUBLANES, N_KEYS, SUBLANES, N_KEYS), lambda i: (i, 0, 0, 0)),
        out_shape=jax.ShapeDtypeStruct((n // SUBLANES, N_KEYS, SUBLANES, N_KEYS), F32),
        compiler_params=_cparams(("parallel",)), name="peer_weights",
    )(i1, i2, g)


def _peer_dense_kernel(x_ref, wt_ref, u_ref, v_ref, y_ref, z_ref, *, rows_per_step):
    c = pl.program_id(1)
    tt = x_ref.shape[0]

    @pl.when(c == 0)
    def _():
        y_ref[...] = jnp.zeros(y_ref.shape, F32)

    hid = _nt(x_ref[...], u_ref[...])
    for a in range(rows_per_step):
        h = hid[:, a * N_KEYS:(a + 1) * N_KEYS]
        w = wt_ref[:, a].reshape(tt, N_KEYS)
        gelu = 0.5 * h * (1.0 + lax.erf(h * math.sqrt(0.5)))
        z_ref[:, a * N_KEYS:(a + 1) * N_KEYS] = (w * gelu).astype(BF16)
    y_ref[...] += _mm(z_ref[...], v_ref[...])


def _peer_dense(x2d, wt, u, v, tt, rows_per_step):
    n = x2d.shape[0]
    ne = rows_per_step * N_KEYS
    return pl.pallas_call(
        functools.partial(_peer_dense_kernel, rows_per_step=rows_per_step),
        grid=(n // tt, N_KEYS // rows_per_step),
        in_specs=[pl.BlockSpec((tt, D_MODEL), lambda i, c: (i, 0)),
                  pl.BlockSpec((tt // SUBLANES, rows_per_step, SUBLANES, N_KEYS), lambda i, c: (i, c, 0, 0)),
                  pl.BlockSpec((ne, D_MODEL), lambda i, c: (c, 0)),
                  pl.BlockSpec((ne, D_MODEL), lambda i, c: (c, 0))],
        out_specs=pl.BlockSpec((tt, D_MODEL), lambda i, c: (i, 0)),
        out_shape=jax.ShapeDtypeStruct((n, D_MODEL), F32),
        scratch_shapes=[pltpu.VMEM((tt, ne), BF16)],
        compiler_params=_cparams(("parallel", "arbitrary")), name="peer_dense",
    )(x2d, wt, u, v)


def _ple_kernel(x1_ref, y_ref, p_ref, l2g_ref, l2b_ref, wg_ref, wp_ref, l3g_ref, l3b_ref, o_ref):
    x2 = _layer_norm(ALPHA * x1_ref[...] + y_ref[...], l2g_ref[...], l2b_ref[...])
    gate = _sigmoid(_mm(x2.astype(BF16), wg_ref[...]))
    ple = _mm(p_ref[...].astype(BF16), wp_ref[...]) * gate
    o_ref[...] = _layer_norm(ALPHA * x2 + ple, l3g_ref[...], l3b_ref[...])


def _ple(x1, y, p_emb, l2g, l2b, wg, wp, l3g, l3b, tm):
    n = x1.shape[0]
    row = lambda c: pl.BlockSpec((tm, c), lambda i: (i, 0))
    vec = _resident((1, D_MODEL))
    return pl.pallas_call(
        _ple_kernel, grid=(n // tm,),
        in_specs=[row(D_MODEL), row(D_MODEL), row(PLE_DIM), vec, vec, _resident(wg.shape), _resident(wp.shape), vec, vec],
        out_specs=row(D_MODEL),
        out_shape=jax.ShapeDtypeStruct((n, D_MODEL), F32),
        compiler_params=_cparams(("parallel",)), name="ple_ln2_ln3",
    )(x1, y, p_emb, l2g, l2b, wg, wp, l3g, l3b)


def _rope_tables(pos):
    inv = ROPE_THETA ** (-jnp.arange(0, QK_ROPE, 2, dtype=F32) / QK_ROPE)
    ang = pos.astype(F32)[:, None] * inv[None, :]
    cos, sin = jnp.cos(ang), jnp.sin(ang)
    ck = jnp.concatenate([cos, cos], axis=-1)
    sk = jnp.concatenate([-sin, sin], axis=-1)
    return jnp.tile(ck, (1, MLA_HEADS)), jnp.tile(sk, (1, MLA_HEADS)), ck, sk


def _prepare_weights(w_in, kv_norm_g, w_uk, w_uv, rw_mu, rw_w0, rw_w_up, rw_a0, rw_a_up, rw_g_up,
                     rw_k_k, rw_k_a, rw_r_k, rw_gn_g, rw_gn_b, w_o, ln1_g, ln1_b,
                     peer_wq, peer_k1, peer_k2, peer_u, peer_v, ln2_g, ln2_b,
                     ple_w, ple_gate_w, ln3_g, ln3_b):
    half = QK_ROPE // 2
    swap = lambda w: jnp.concatenate([w[..., half:], w[..., :half]], axis=-1)
    wq = w_in[:, :Q_COLS].reshape(D_MODEL, MLA_HEADS, QK_NOPE + QK_ROPE)
    wq_n = wq[:, :, :QK_NOPE].reshape(D_MODEL, -1)
    wq_r = wq[:, :, QK_NOPE:]
    w_c = w_in[:, Q_COLS:Q_COLS + KV_RANK]
    w_kr = w_in[:, Q_COLS + KV_RANK:MLA_COLS]
    w_mla = jnp.concatenate([wq_n, wq_r.reshape(D_MODEL, -1), swap(wq_r).reshape(D_MODEL, -1),
                             w_c, w_kr, swap(w_kr)], axis=1).astype(BF16)
    row = lambda a: a.reshape(1, -1)
    head_of_lane = jnp.arange(RW_WIDTH) // RW_HEAD
    e = (head_of_lane[:, None] == jnp.arange(LANES)[None, :]).astype(BF16)
    half_of = (jnp.arange(2 * LANES) % LANES) // RW_HEAD
    e2 = (half_of[:, None] == half_of[None, :LANES]).astype(BF16)
    return dict(
        e2=e2,
        w_mla=w_mla, kv_g=row(kv_norm_g),
        w_uk2=w_uk.reshape(KV_RANK, -1).astype(BF16), w_uv2=w_uv.reshape(KV_RANK, -1).astype(BF16),
        w_ukt=jnp.transpose(w_uk, (1, 2, 0)).astype(BF16),
        rw=(w_in[:, MLA_COLS:].astype(BF16), row(rw_mu), row(rw_w0), rw_w_up.astype(BF16), row(rw_a0),
            rw_a_up.astype(BF16), rw_g_up.astype(BF16), row(rw_k_k), row(rw_k_a), row(rw_r_k), e, e.T),
        gn_g=row(rw_gn_g), gn_b=row(rw_gn_b), e=e, et=e.T,
        wo1=w_o[:MLA_HEADS * V_HEAD].astype(BF16), wo2=w_o[MLA_HEADS * V_HEAD:].astype(BF16),
        ln1_g=row(ln1_g), ln1_b=row(ln1_b),
        peer_wq=peer_wq.astype(BF16), peer_k1=peer_k1.astype(BF16), peer_k2=peer_k2.astype(BF16),
        peer_u=peer_u.astype(BF16), peer_v=peer_v.astype(BF16),
        ln2_g=row(ln2_g), ln2_b=row(ln2_b), ple_w=ple_w.astype(BF16), ple_gate_w=ple_gate_w.astype(BF16),
        ln3_g=row(ln3_g), ln3_b=row(ln3_b))


def _tile(n, pref):
    t = min(n, pref)
    while n % t:
        t -= 8
    return t


def _channel_mix(x2d, o_mla, raw, g, bonus, p_emb, w):
    n = x2d.shape[0]
    x1, x1b = _mix_out(x2d, o_mla, raw, g, bonus, w["gn_g"], w["gn_b"], w["e"], w["et"], w["wo1"], w["wo2"],
                       w["ln1_g"], w["ln1_b"], _tile(n, 256))
    i1, i2, gate = _peer_route(x1b, w["peer_wq"], w["peer_k1"], w["peer_k2"], _tile(n, 256))
    wt = _peer_weights(i1, i2, gate, _tile(n, 128))
    y = _peer_dense(x1b, wt, w["peer_u"], w["peer_v"], _tile(n, PEER_TOKENS), PEER_KEYS_PER_STEP)
    return _ple(x1, y, p_emb, w["ln2_g"], w["ln2_b"], w["ple_gate_w"], w["ple_w"], w["ln3_g"], w["ln3_b"],
                _tile(n, 256))


def _layer(xp, xs, pp, ps, past_len, cache_ckv, cache_krope, page_table, wkv0, shift0, w):
    b, t, _ = xp.shape
    db, ts, _ = xs.shape
    assert ts == 1, "the sample path handles one new token per sequence"
    n = b * t
    r3 = lambda a: a.reshape(b, t, -1)
    f2 = lambda a: a.reshape(n, -1)

    xp2 = xp.reshape(n, D_MODEL)
    cq, sq, ck, sk = (jnp.tile(a, (b, 1)) for a in _rope_tables(jnp.arange(t)))
    qc, kc, ckv_p, kr_p, v = _mla_proj(xp2, w["w_mla"], w["kv_g"], cq, sq, ck, sk,
                                       (w["w_uk2"], w["w_uv2"]), True, _tile(n, 256))
    o_mla_p = _mla_prompt_attention(r3(qc), r3(kc), r3(v), _tile(t, 512), _tile(t, 512))
    r, dec, k, vv, nkk, bb, g_p, bonus_p, last_p = _rwkv_proj_prompt(xp, w["rw"], _tile(t, 256))
    s0 = jnp.zeros((b, RW_HEADS, RW_HEAD, RW_HEAD), F32)

    xs2 = xs.reshape(db, D_MODEL)
    cq, sq, ck, sk = (jnp.tile(a, (db, 1)) for a in _rope_tables(jnp.full((1,), past_len)))
    ql, qr_s, ckv_s, kr_s = _mla_proj(xs2, w["w_mla"], w["kv_g"], cq, sq, ck, sk, (w["w_ukt"],), False,
                                      _tile(db, 128))
    decode_args = (page_table, ql.reshape(db, MLA_HEADS, KV_RANK), qr_s.reshape(db, MLA_HEADS, QK_ROPE),
                   ckv_s.reshape(db, 1, KV_RANK), kr_s.reshape(db, 1, QK_ROPE),
                   cache_ckv, jnp.swapaxes(cache_krope, 1, 2), math.gcd(page_table.shape[1], DECODE_PAGES))
    if t == SCAN_GROUP * db * (page_table.shape[1] // decode_args[-1]):
        o_lat, raw_p, wkv_p = _decode_and_scan(*decode_args, s0, r, dec, k, vv, nkk, bb, w["e2"])
    else:
        o_lat = _mla_sample_attention(*decode_args)
        raw_p, wkv_p = _rwkv_scan(s0, r, dec, k, vv, nkk, bb, w["e2"], b, _tile(t, 128))
    yp = _channel_mix(xp2, f2(o_mla_p), f2(raw_p), f2(g_p), f2(bonus_p), pp.reshape(n, -1), w)

    o_mla_s = _head_up(o_lat.reshape(db, MLA_HEADS * KV_RANK), w["w_uv2"])
    r, dec, k, vv, nkk, bb, g_s, bonus_s, last_s = _rwkv_proj_sample(xs2, shift0, w["rw"])
    s3 = lambda a: a.reshape(db, 1, RW_WIDTH)
    raw_s, wkv_s = _rwkv_scan(wkv0, s3(r), s3(dec), s3(k), s3(vv), s3(nkk), s3(bb), w["e2"], _tile(db, 8), 1)
    ys = _channel_mix(xs2, o_mla_s, raw_s.reshape(db, RW_WIDTH), g_s, bonus_s, ps.reshape(db, -1), w)
    return ((yp.reshape(b, t, D_MODEL), r3(ckv_p), r3(kr_p), wkv_p, last_p.reshape(b, RW_COLS)),
            (ys.reshape(db, 1, D_MODEL), ckv_s.reshape(db, 1, KV_RANK), kr_s.reshape(db, 1, QK_ROPE), wkv_s, last_s))


def kernel(x_prompt, x_sample, p_prompt, p_sample, cache_ckv, cache_krope, state_wkv, state_shift, page_table, w_in, kv_norm_g, w_uk, w_uv, rw_mu, rw_w0, rw_w_up, rw_a0, rw_a_up, rw_g_up, rw_k_k, rw_k_a, rw_r_k, rw_gn_g, rw_gn_b, w_o, ln1_g, ln1_b, peer_wq, peer_k1, peer_k2, peer_u, peer_v, ln2_g, ln2_b, ple_w, ple_gate_w, ln3_g, ln3_b):
    layer_params = (w_in, kv_norm_g, w_uk, w_uv, rw_mu, rw_w0, rw_w_up, rw_a0, rw_a_up, rw_g_up,
                    rw_k_k, rw_k_a, rw_r_k, rw_gn_g, rw_gn_b, w_o, ln1_g, ln1_b,
                    peer_wq, peer_k1, peer_k2, peer_u, peer_v, ln2_g, ln2_b,
                    ple_w, ple_gate_w, ln3_g, ln3_b)
    depth = w_in.shape[0]
    past_len = page_table.shape[1] * cache_ckv.shape[2]
    xp, xs = x_prompt, x_sample
    outs_p, outs_s = [], []
    for i in range(depth):
        w = _prepare_weights(*(a[i] for a in layer_params))
        (xp, *rest_p), (xs, *rest_s) = _layer(xp, xs, p_prompt[i], p_sample[i], past_len, cache_ckv[i],
                                              cache_krope[i], page_table, state_wkv[i], state_shift[i], w)
        outs_p.append(rest_p)
        outs_s.append(rest_s)
    stack = lambda outs, j: jnp.stack([o[j] for o in outs])
    return (xp, xs, stack(outs_p, 0), stack(outs_p, 1), stack(outs_p, 2), stack(outs_p, 3),
            stack(outs_s, 0), stack(outs_s, 1), stack(outs_s, 2), stack(outs_s, 3))
```
